```python
import math
import jax, jax.numpy as jnp
from jax import lax
import numpy as np

D_MODEL = 1024
BATCH = 8
SEQ = 2048
DEPTH = 4
DEC_BATCH = 32
DEC_SEQ = 1
PAST_LEN = 8192
PAGE_SIZE = 128

N_A_LAYERS = DEPTH // 2
N_B_LAYERS = DEPTH - N_A_LAYERS
D_INNER = 2 * D_MODEL
SSM_HEAD_DIM = 64
SSM_HEADS = D_INNER // SSM_HEAD_DIM
SSM_GROUPS = 4
SSM_STATE = 128
CONV_WIDTH = 4
CONV_DIM = D_INNER + 2 * SSM_GROUPS * SSM_STATE
SSD_CHUNK = 128
A_IN_PROJ = D_INNER + CONV_DIM + SSM_HEADS
DIL_GROUPS = ((128, 1), (512, 4), (2048, 16))
N_DIL = len(DIL_GROUPS)
ATT_HEAD_DIM = 64
ATT_HEADS = D_MODEL // ATT_HEAD_DIM
ATT_WIDTH = ATT_HEADS * ATT_HEAD_DIM
Q_WIDTH = N_DIL * ATT_WIDTH
B_IN_PROJ = Q_WIDTH + ATT_WIDTH
KV_WIDTH = 2 * N_DIL * ATT_WIDTH
Q_BLOCK = 128
ATT_SCALE = ATT_HEAD_DIM ** -0.5
LN_EPS = 1e-5
RMS_EPS = 1e-5
DEEPNORM_ALPHA = (2.0 * DEPTH) ** 0.25
DEEPNORM_BETA = (8.0 * DEPTH) ** -0.25

kernel_name = "yoco_ssd_dilated_alibi_deepnorm_step"


def layer_norm(x, g, b):
    xf = x.astype(jnp.float32)
    mu = jnp.mean(xf, -1, keepdims=True)
    var = jnp.mean(jnp.square(xf - mu), -1, keepdims=True)
    return ((xf - mu) * lax.rsqrt(var + LN_EPS) * g.astype(jnp.float32) + b.astype(jnp.float32)).astype(x.dtype)


def gated_group_rmsnorm(y, z, w):
    h = (y * jax.nn.silu(z)).astype(jnp.float32)
    hg = h.reshape(h.shape[:-1] + (SSM_GROUPS, D_INNER // SSM_GROUPS))
    hg = hg * lax.rsqrt(jnp.mean(hg * hg, -1, keepdims=True) + RMS_EPS)
    return (hg.reshape(h.shape) * w.astype(jnp.float32)).astype(y.dtype)


def causal_dwconv(u, prev, w, b):
    ext = jnp.concatenate([prev.astype(u.dtype), u], axis=1)
    out = lax.conv_general_dilated(ext, w.astype(ext.dtype)[:, None, :], window_strides=(1,), padding='VALID',
                                   dimension_numbers=('NWC', 'WIO', 'NWC'), feature_group_count=u.shape[-1])
    return out + b.astype(out.dtype), ext[:, -(CONV_WIDTH - 1):]


def ssd_scan(x, dt, a, bm, cm, h0):
    bsz, seq_len, n_heads, p = x.shape
    g, n = bm.shape[-2:]
    hpg = n_heads // g
    t = min(SSD_CHUNK, seq_len)
    nc = -(-seq_len // t)
    pad = nc * t - seq_len

    def chunk(u):
        u = jnp.pad(u.astype(jnp.float32), [(0, 0), (0, pad)] + [(0, 0)] * (u.ndim - 2))
        return u.reshape((bsz, nc, t) + u.shape[2:])

    x, dt, bm, cm = chunk(x), chunk(dt), chunk(bm), chunk(cm)
    xdt = x * dt[..., None]
    a_cs = jnp.cumsum(dt * a.astype(jnp.float32), axis=2)
    a_cs_h = jnp.moveaxis(a_cs, -1, 2)
    causal = jnp.tril(jnp.ones((t, t), dtype=bool))
    seg = jnp.where(causal, a_cs_h[..., :, None] - a_cs_h[..., None, :], -jnp.inf)
    decay = jnp.exp(seg).reshape(bsz, nc, g, hpg, t, t)
    cb = jnp.einsum('bclgn,bcsgn->bcgls', cm, bm)
    xdt_g = xdt.reshape(bsz, nc, t, g, hpg, p)
    y_diag = jnp.einsum('bcgls,bcgkls,bcsgkp->bclgkp', cb, decay, xdt_g)
    to_end = jnp.exp(a_cs[:, :, -1:, :] - a_cs)
    states = jnp.einsum('bclgn,bclgkp->bcgkpn', bm,
                        (xdt * to_end[..., None]).reshape(bsz, nc, t, g, hpg, p))
    states = states.reshape(bsz, nc, n_heads, p, n)
    chunk_decay = jnp.exp(a_cs[:, :, -1, :])

    def step(h, inp):
        dec, s_c = inp
        return h * dec[..., None, None] + s_c, h

    h_final, h_in = lax.scan(step, h0.astype(jnp.float32),
                             (jnp.moveaxis(chunk_decay, 1, 0), jnp.moveaxis(states, 1, 0)))
    h_in = jnp.moveaxis(h_in, 0, 1).reshape(bsz, nc, g, hpg, p, n)
    y_off = jnp.einsum('bclgn,bcgkpn->bclgkp', cm, h_in) * jnp.exp(a_cs).reshape(bsz, nc, t, g, hpg)[..., None]
    y = (y_diag + y_off).reshape(bsz, nc * t, n_heads, p)[:, :seq_len]
    return y, h_final


def mamba2_mixer(u, conv_prev, ssm_prev, w_in, conv_w, conv_b, dt_bias, a_log, d_skip, norm_w, w_out):
    bsz, seq_len, _ = u.shape
    zxbcdt = u @ w_in
    z, xbc, dt = jnp.split(zxbcdt, [D_INNER, D_INNER + CONV_DIM], axis=-1)
    xbc, conv_new = causal_dwconv(xbc, conv_prev, conv_w, conv_b)
    xbc = jax.nn.silu(xbc)
    xs, bm, cm = jnp.split(xbc, [D_INNER, D_INNER + SSM_GROUPS * SSM_STATE], axis=-1)
    xs = xs.reshape(bsz, seq_len, SSM_HEADS, SSM_HEAD_DIM)
    bm = bm.reshape(bsz, seq_len, SSM_GROUPS, SSM_STATE)
    cm = cm.reshape(bsz, seq_len, SSM_GROUPS, SSM_STATE)
    dt = jax.nn.softplus(dt.astype(jnp.float32) + dt_bias.astype(jnp.float32))
    a = -jnp.exp(a_log.astype(jnp.float32))
    y, ssm_new = ssd_scan(xs, dt, a, bm, cm, ssm_prev)
    y = y + xs.astype(jnp.float32) * d_skip.astype(jnp.float32)[:, None]
    y = gated_group_rmsnorm(y.reshape(bsz, seq_len, D_INNER).astype(u.dtype), z, norm_w)
    return y @ w_out, conv_new, ssm_new


def dilated_group_prompt(q, k, v, window, dil, slopes):
    bsz, seq_len, nh, hd = q.shape
    n = seq_len // dil
    w = window // dil
    qb = min(Q_BLOCK, n)
    nblk = -(-n // qb)
    n_pad = nblk * qb

    def by_residue(u, left):
        u = u.reshape(bsz, n, dil, nh, hd).transpose(0, 2, 1, 3, 4)
        return jnp.pad(u, ((0, 0), (0, 0), (left, n_pad - n), (0, 0), (0, 0)))

    qr = by_residue(q, 0).reshape(bsz, dil, nblk, qb, nh, hd)
    kr = by_residue(k, w)
    vr = by_residue(v, w)
    key_idx = jnp.arange(nblk)[:, None] * qb + jnp.arange(qb + w)[None, :]
    kb = kr[:, :, key_idx]
    vb = vr[:, :, key_idx]
    qi = jnp.arange(nblk)[:, None] * qb + jnp.arange(qb)[None, :]
    kj = key_idx - w
    steps = qi[:, :, None] - kj[:, None, :]
    valid = (steps >= 0) & (steps <= w) & (kj[:, None, :] >= 0)
    dist = (steps * dil).astype(jnp.float32)
    bias = jnp.where(valid[:, None], -slopes[None, :, None, None] * dist[:, None], -jnp.inf)
    scores = jnp.einsum('brcqhd,brckhd->brchqk', qr, kb,
                        preferred_element_type=jnp.float32) * ATT_SCALE + bias
    m = jnp.max(scores, -1)
    pr = jnp.exp(scores - m[..., None])
    s = jnp.sum(pr, -1)
    m = jnp.moveaxis(m, 3, 4)
    s = jnp.moveaxis(s, 3, 4)
    o = jnp.einsum('brchqk,brckhd->brcqhd', pr, vb.astype(jnp.float32)) / s[..., None]

    def back(u):
        u = u.reshape((bsz, dil, n_pad) + u.shape[4:])[:, :, :n]
        u = jnp.swapaxes(u, 1, 2)
        return u.reshape((bsz, seq_len) + u.shape[3:])

    return back(o), back(m), back(s)


def dilated_group_sample(q, k_all, v_all, buf_len, window, dil, slopes):
    n_new = q.shape[1]
    steps = jnp.arange(window // dil + 1)
    idx = buf_len + jnp.arange(n_new)[:, None] - steps[None, :] * dil
    valid = idx >= 0
    idx_c = jnp.maximum(idx, 0)
    kg = k_all[:, idx_c]
    vg = v_all[:, idx_c]
    dist = (steps * dil).astype(jnp.float32)
    bias = jnp.where(valid[:, None, :], -slopes[None, :, None] * dist[None, None, :], -jnp.inf)
    scores = jnp.einsum('bthd,btkhd->bthk', q, kg, preferred_element_type=jnp.float32) * ATT_SCALE + bias
    m = jnp.max(scores, -1)
    pr = jnp.exp(scores - m[..., None])
    s = jnp.sum(pr, -1)
    o = jnp.einsum('bthk,btkhd->bthd', pr, vg.astype(jnp.float32)) / s[..., None]
    return o, m, s


def b_project(h, w_in):
    bsz, seq_len, _ = h.shape
    proj = h @ w_in
    q = proj[..., :Q_WIDTH].reshape(bsz, seq_len, N_DIL, ATT_HEADS, ATT_HEAD_DIM)
    return q, proj[..., Q_WIDTH:]


def b_finish(results, gate, w_out):
    o = jnp.stack([r[0] for r in results])
    m = jnp.stack([r[1] for r in results])
    s = jnp.stack([r[2] for r in results])
    wts = s * jnp.exp(m - jnp.max(m, 0))
    o = jnp.einsum('gblh,gblhd->blhd', wts, o) / jnp.sum(wts, 0)[..., None]
    bsz, seq_len = gate.shape[:2]
    o = o.reshape(bsz, seq_len, ATT_WIDTH).astype(gate.dtype) * jax.nn.silu(gate)
    return o @ w_out


def dilated_mixer_prompt(h, kv_groups, w_in, w_out, slopes):
    q, gate = b_project(h, w_in)
    res = [dilated_group_prompt(q[:, :, g], kv_groups[g][:, :, 0], kv_groups[g][:, :, 1], win, dil, slopes)
           for g, (win, dil) in enumerate(DIL_GROUPS)]
    return b_finish(res, gate, w_out)


def dilated_mixer_sample(h, kv_full, buf_lens, w_in, w_out, slopes):
    q, gate = b_project(h, w_in)
    res = [dilated_group_sample(q[:, :, g], kv_full[g][:, :, 0], kv_full[g][:, :, 1], buf_lens[g], win, dil, slopes)
           for g, (win, dil) in enumerate(DIL_GROUPS)]
    return b_finish(res, gate, w_out)


def setup_inputs(seed: int = 0) -> dict:
    key = jax.random.key(seed)
    ks = jax.random.split(key, 20)
    f32 = jnp.float32

    def nrm(k, shape, scale):
        return jax.random.normal(k, shape, f32) * scale

    buf = [min(w, PAST_LEN) for w, _ in DIL_GROUPS]
    kv_shape = lambda L: (DEC_BATCH, L, 2, ATT_HEADS, ATT_HEAD_DIM)
    dt0 = jnp.exp(jax.random.uniform(ks[10], (N_A_LAYERS, SSM_HEADS), f32, math.log(1e-3), math.log(1e-1)))
    kv_col_scale = jnp.concatenate([jnp.ones((N_DIL * ATT_WIDTH,), f32),
                                    jnp.full((N_DIL * ATT_WIDTH,), DEEPNORM_BETA, f32)])
    return {
        "x_prompt": nrm(ks[0], (BATCH, SEQ, D_MODEL), 1.0),
        "x_sample": nrm(ks[1], (DEC_BATCH, DEC_SEQ, D_MODEL), 1.0),
        "state_ssm": nrm(ks[2], (N_A_LAYERS, DEC_BATCH, SSM_HEADS, SSM_HEAD_DIM, SSM_STATE), 0.5),
        "state_conv": nrm(ks[3], (N_A_LAYERS, DEC_BATCH, CONV_WIDTH - 1, CONV_DIM), 1.0),
        "cache_kv_w128": nrm(ks[4], kv_shape(buf[0]), 1.0),
        "cache_kv_w512": nrm(ks[5], kv_shape(buf[1]), 1.0),
        "cache_kv_w2048": nrm(ks[6], kv_shape(buf[2]), 1.0),
        "a_in_proj": nrm(ks[7], (N_A_LAYERS, D_MODEL, A_IN_PROJ), D_MODEL ** -0.5),
        "a_conv_w": nrm(ks[8], (N_A_LAYERS, CONV_WIDTH, CONV_DIM), CONV_WIDTH ** -0.5),
        "a_conv_b": nrm(ks[9], (N_A_LAYERS, CONV_DIM), 0.02),
        "a_dt_bias": dt0 + jnp.log(-jnp.expm1(-dt0)),
        "a_log": jnp.log(jax.random.uniform(ks[11], (N_A_LAYERS, SSM_HEADS), f32, 1.0, 16.0)),
        "a_d": 1.0 + nrm(ks[12], (N_A_LAYERS, SSM_HEADS), 0.02),
        "a_norm_w": 1.0 + nrm(ks[13], (N_A_LAYERS, D_INNER), 0.02),
        "a_out_proj": nrm(ks[14], (N_A_LAYERS, D_INNER, D_MODEL), D_INNER ** -0.5 * DEEPNORM_BETA),
        "kv_proj": nrm(ks[15], (D_MODEL, KV_WIDTH), D_MODEL ** -0.5) * kv_col_scale,
        "b_in_proj": nrm(ks[16], (N_B_LAYERS, D_MODEL, B_IN_PROJ), D_MODEL ** -0.5),
        "b_out_proj": nrm(ks[17], (N_B_LAYERS, ATT_WIDTH, D_MODEL), ATT_WIDTH ** -0.5 * DEEPNORM_BETA),
        "ln_g": 1.0 + nrm(ks[18], (DEPTH, D_MODEL), 0.02),
        "ln_b": nrm(ks[19], (DEPTH, D_MODEL), 0.02),
    }


def reference(x_prompt, x_sample, state_ssm, state_conv, cache_kv_w128, cache_kv_w512, cache_kv_w2048,
              a_in_proj, a_conv_w, a_conv_b, a_dt_bias, a_log, a_d, a_norm_w, a_out_proj,
              kv_proj, b_in_proj, b_out_proj, ln_g, ln_b):
    bsz_p, seq_p, _ = x_prompt.shape
    bsz_s, seq_s, _ = x_sample.shape
    slopes = jnp.exp2(-8.0 * jnp.arange(1, ATT_HEADS + 1, dtype=jnp.float32) / ATT_HEADS)
    caches = (cache_kv_w128, cache_kv_w512, cache_kv_w2048)
    buf_lens = [c.shape[1] for c in caches]
    hp, hs = x_prompt, x_sample
    ssm_p_list, conv_p_list, ssm_s_list, conv_s_list = [], [], [], []
    kv_groups_p, kv_full_s, new_kv_p, new_kv_s = [], [], [], []
    for layer in range(DEPTH):
        if layer < N_A_LAYERS:
            i = layer
            params = (a_in_proj[i], a_conv_w[i], a_conv_b[i], a_dt_bias[i], a_log[i], a_d[i], a_norm_w[i], a_out_proj[i])
            conv0 = jnp.zeros((bsz_p, CONV_WIDTH - 1, CONV_DIM), hp.dtype)
            ssm0 = jnp.zeros((bsz_p, SSM_HEADS, SSM_HEAD_DIM, SSM_STATE), jnp.float32)
            dp, conv_p, ssm_p = mamba2_mixer(hp, conv0, ssm0, *params)
            ds, conv_s, ssm_s = mamba2_mixer(hs, state_conv[i], state_ssm[i], *params)
            conv_p_list.append(conv_p)
            ssm_p_list.append(ssm_p)
            conv_s_list.append(conv_s.astype(state_conv.dtype))
            ssm_s_list.append(ssm_s.astype(state_ssm.dtype))
        else:
            if layer == N_A_LAYERS:
                kv_p = (hp @ kv_proj).reshape(bsz_p, seq_p, 2, N_DIL, ATT_HEADS, ATT_HEAD_DIM)
                kv_s = (hs @ kv_proj).reshape(bsz_s, seq_s, 2, N_DIL, ATT_HEADS, ATT_HEAD_DIM)
                for g, ((win, dil), cache) in enumerate(zip(DIL_GROUPS, caches)):
                    rows_p = kv_p[:, :, :, g]
                    kv_groups_p.append(rows_p)
                    new_kv_p.append(rows_p[:, -min(win, seq_p):])
                    rows_s = kv_s[:, :, :, g]
                    full = jnp.concatenate([cache.astype(rows_s.dtype), rows_s], axis=1)
                    kv_full_s.append(full)
                    new_kv_s.append(full[:, -min(win, full.shape[1]):].astype(cache.dtype))
            j = layer - N_A_LAYERS
            dp = dilated_mixer_prompt(hp, kv_groups_p, b_in_proj[j], b_out_proj[j], slopes)
            ds = dilated_mixer_sample(hs, kv_full_s, buf_lens, b_in_proj[j], b_out_proj[j], slopes)
        hp = layer_norm(DEEPNORM_ALPHA * hp + dp, ln_g[layer], ln_b[layer])
        hs = layer_norm(DEEPNORM_ALPHA * hs + ds, ln_g[layer], ln_b[layer])
    ssm_prompt = jnp.stack(ssm_p_list)
    conv_prompt = jnp.stack(conv_p_list)
    ssm_sample = jnp.stack(ssm_s_list)
    conv_sample = jnp.stack(conv_s_list)
    kv_w128_prompt, kv_w512_prompt, kv_w2048_prompt = new_kv_p
    kv_w128_sample, kv_w512_sample, kv_w2048_sample = new_kv_s
    return (hp, hs, ssm_prompt, conv_prompt, kv_w128_prompt, kv_w512_prompt, kv_w2048_prompt,
            ssm_sample, conv_sample, kv_w128_sample, kv_w512_sample, kv_w2048_sample)
```

```python
import functools

import jax
import jax.numpy as jnp
from jax import lax
from jax.experimental import pallas as pl
from jax.experimental.pallas import tpu as pltpu

F32 = jnp.float32
BF16 = jnp.bfloat16

D_MODEL = 1024
N_A_LAYERS = 2
N_B_LAYERS = 2
D_INNER = 2048
SSM_HEAD_DIM = 64
SSM_HEADS = 32
SSM_GROUPS = 4
SSM_STATE = 128
HEADS_PER_GROUP = SSM_HEADS // SSM_GROUPS
GROUP_WIDTH = D_INNER // SSM_GROUPS
CONV_WIDTH = 4
BC_WIDTH = 2 * SSM_GROUPS * SSM_STATE
CONV_DIM = D_INNER + BC_WIDTH
CHUNK = 128
DIL_GROUPS = ((128, 1), (512, 4), (2048, 16))
N_DIL = 3
ATT_HEADS = 16
ATT_HEAD_DIM = 64
ATT_WIDTH = 1024
Q_BLOCK = 128
ATT_SCALE = ATT_HEAD_DIM ** -0.5
LN_EPS = 1e-5
RMS_EPS = 1e-5
DEEPNORM_ALPHA = (2.0 * 4) ** 0.25

LANES = 128
SUBLANES = 8
VMEM_LIMIT = 48 * 1024 * 1024


def _cparams(n_grid):
    return pltpu.CompilerParams(dimension_semantics=("arbitrary",) * n_grid, vmem_limit_bytes=VMEM_LIMIT)


def _silu(x):
    return x * (1.0 / (1.0 + jnp.exp(-x)))


def _softplus(x):
    return jnp.maximum(x, 0.0) + jnp.log1p(jnp.exp(-jnp.abs(x)))


def _split3(a):
    hi = a.astype(BF16)
    r1 = a - hi.astype(F32)
    mid = r1.astype(BF16)
    lo = (r1 - mid.astype(F32)).astype(BF16)
    return hi, mid, lo


def _dot(a, b):
    return jnp.dot(a, b, preferred_element_type=F32)


def _dot_nt(a, b):
    return lax.dot_general(a, b, (((1,), (1,)), ((), ())), preferred_element_type=F32)


def _exact_dot(a, m01):
    hi, mid, lo = _split3(a)
    return _dot(hi, m01) + _dot(mid, m01) + _dot(lo, m01)


def _exact_dot_left(m01, a):
    hi, mid, lo = _split3(a)
    return _dot(m01, hi) + _dot(m01, mid) + _dot(m01, lo)


def _layer_norm(v, g, b):
    mu = jnp.mean(v, axis=-1, keepdims=True)
    d = v - mu
    var = jnp.mean(d * d, axis=-1, keepdims=True)
    return d * lax.rsqrt(var + LN_EPS) * g + b


def _mm_kernel(x_ref, w_ref, o_ref, xb_ref):
    @pl.when(pl.program_id(1) == 0)
    def _():
        xb_ref[...] = x_ref[...].astype(BF16)

    o_ref[...] = _dot(xb_ref[...], w_ref[...]).astype(o_ref.dtype)


def _matmul(x, w, *, tn, out_dtype=F32, tm=1024):
    m, k = x.shape
    n = w.shape[1]
    tm = min(tm, m)
    assert m % tm == 0 and n % tn == 0
    return pl.pallas_call(
        _mm_kernel,
        grid=(m // tm, n // tn),
        in_specs=[pl.BlockSpec((tm, k), lambda i, j: (i, 0)),
                  pl.BlockSpec((k, tn), lambda i, j: (0, j))],
        out_specs=pl.BlockSpec((tm, tn), lambda i, j: (i, j)),
        out_shape=jax.ShapeDtypeStruct((m, n), out_dtype),
        scratch_shapes=[pltpu.VMEM((tm, k), BF16)],
        compiler_params=_cparams(2),
    )(x, w)


def _mm_ln_kernel(y_ref, w_ref, r_ref, g_ref, b_ref, o_ref):
    acc = _dot(y_ref[...].astype(BF16), w_ref[...])
    v = DEEPNORM_ALPHA * r_ref[...] + acc
    o_ref[...] = _layer_norm(v, g_ref[...], b_ref[...])


def _matmul_ln(y, w, resid, g, b, *, tm=512):
    m, k = y.shape
    n = w.shape[1]
    tm = min(tm, m)
    assert m % tm == 0
    return pl.pallas_call(
        _mm_ln_kernel,
        grid=(m // tm,),
        in_specs=[pl.BlockSpec((tm, k), lambda i: (i, 0)),
                  pl.BlockSpec((k, n), lambda i: (0, 0)),
                  pl.BlockSpec((tm, n), lambda i: (i, 0)),
                  pl.BlockSpec((1, n), lambda i: (0, 0)),
                  pl.BlockSpec((1, n), lambda i: (0, 0))],
        out_specs=pl.BlockSpec((tm, n), lambda i: (i, 0)),
        out_shape=jax.ShapeDtypeStruct((m, n), F32),
        compiler_params=_cparams(1),
    )(y, w, resid, g, b)


def _ssd_prompt_kernel(z_ref, xr_ref, bcr_ref, dtr_ref, cwx_ref, cwbc_ref, cbx_ref, cbbc_ref,
                       dtb_ref, alog_ref, dexp_ref, nw_ref, e_ref, tril_ref,
                       y_ref, ssm_ref, conv_ref,
                       st_ref, extx_ref, extbc_ref):
    c = pl.program_id(1)
    t = CHUNK

    @pl.when(c == 0)
    def _():
        st_ref[...] = jnp.zeros_like(st_ref)
        extx_ref[0:SUBLANES, :] = jnp.zeros((SUBLANES, D_INNER), F32)
        extbc_ref[0:SUBLANES, :] = jnp.zeros((SUBLANES, BC_WIDTH), F32)

    extx_ref[SUBLANES:SUBLANES + t, :] = xr_ref[...]
    extbc_ref[SUBLANES:SUBLANES + t, :] = bcr_ref[...]

    def conv(ext_ref, w_ref, b_ref):
        acc = b_ref[...]
        for k in range(CONV_WIDTH):
            off = SUBLANES - (CONV_WIDTH - 1) + k
            acc = acc + ext_ref[pl.ds(off, t), :] * w_ref[k:k + 1, :]
        return acc

    xs = _silu(conv(extx_ref, cwx_ref, cbx_ref))
    bc = _silu(conv(extbc_ref, cwbc_ref, cbbc_ref))
    tail = t + SUBLANES - (CONV_WIDTH - 1)
    conv_ref[0, :, 0:D_INNER] = extx_ref[pl.ds(tail, CONV_WIDTH - 1), :]
    conv_ref[0, :, D_INNER:CONV_DIM] = extbc_ref[pl.ds(tail, CONV_WIDTH - 1), :]
    extx_ref[0:SUBLANES, :] = extx_ref[t:t + SUBLANES, :]
    extbc_ref[0:SUBLANES, :] = extbc_ref[t:t + SUBLANES, :]

    bm = bc[:, 0:SSM_GROUPS * SSM_STATE]
    cm = bc[:, SSM_GROUPS * SSM_STATE:]

    dt = _softplus(dtr_ref[...] + dtb_ref[...])
    a = -jnp.exp(alog_ref[...])
    acs = _exact_dot_left(tril_ref[...], dt * a)
    acs_t = acs.T
    a_last = acs[t - 1:t, :]
    stacked = jnp.concatenate(
        [dt, jnp.exp(acs), jnp.exp(a_last - acs), jnp.broadcast_to(jnp.exp(a_last), (SUBLANES, LANES))], axis=0)
    ex = _exact_dot(stacked, e_ref[...])
    dt_e = ex[0:t]
    ea_e = ex[t:2 * t]
    te_e = ex[2 * t:3 * t]
    cd_e = ex[3 * t:3 * t + 1]

    xdt = xs * dt_e
    xdt_b = xdt.astype(BF16)
    xw_b = (xdt * te_e).astype(BF16)

    row = lax.broadcasted_iota(jnp.int32, (t, t), 0)
    col = lax.broadcasted_iota(jnp.int32, (t, t), 1)
    causal = row >= col
    lane_lo = lax.broadcasted_iota(jnp.int32, (t, LANES), 1) < SSM_HEAD_DIM

    y_groups = []
    for g in range(SSM_GROUPS):
        g0 = g * GROUP_WIDTH
        cg_b = cm[:, g * SSM_STATE:(g + 1) * SSM_STATE].astype(BF16)
        bg = bm[:, g * SSM_STATE:(g + 1) * SSM_STATE]
        cb = _dot_nt(cg_b, bg.astype(BF16))
        s_old = st_ref[:, g0:g0 + GROUP_WIDTH]
        y_off = _dot(cg_b, s_old.astype(BF16)) * ea_e[:, g0:g0 + GROUP_WIDTH]
        states = _dot(bg.T.astype(BF16), xw_b[:, g0:g0 + GROUP_WIDTH])
        st_ref[:, g0:g0 + GROUP_WIDTH] = s_old * cd_e[:, g0:g0 + GROUP_WIDTH] + states
        pairs = []
        for pr in range(HEADS_PER_GROUP // 2):
            xp = xdt_b[:, g0 + pr * LANES:g0 + (pr + 1) * LANES]
            halves = []
            for half in range(2):
                h = g * HEADS_PER_GROUP + 2 * pr + half
                seg = acs[:, h:h + 1] - acs_t[h:h + 1, :]
                dec = jnp.exp(jnp.where(causal, seg, -jnp.inf))
                halves.append(_dot((cb * dec).astype(BF16), xp))
            pairs.append(jnp.where(lane_lo, halves[0], halves[1]))
        y_groups.append(jnp.concatenate(pairs, axis=1) + y_off)
    y = jnp.concatenate(y_groups, axis=1) + xs * dexp_ref[...]

    hz = y * _silu(z_ref[...])
    normed = []
    for g in range(SSM_GROUPS):
        hg = hz[:, g * GROUP_WIDTH:(g + 1) * GROUP_WIDTH]
        normed.append(hg * lax.rsqrt(jnp.mean(hg * hg, axis=-1, keepdims=True) + RMS_EPS))
    y_ref[...] = (jnp.concatenate(normed, axis=1) * nw_ref[...]).astype(y_ref.dtype)

    @pl.when(c == pl.num_programs(1) - 1)
    def _():
        for j in range(D_INNER // LANES):
            tile = st_ref[:, j * LANES:(j + 1) * LANES].T
            ssm_ref[0, 2 * j:2 * j + 2] = tile.reshape(2, SSM_HEAD_DIM, SSM_STATE)


def _ssd_prompt(zx, dtp, prm, bsz, seq):
    nc = seq // CHUNK
    m = bsz * seq
    row = lambda b, c: b * nc + c
    const = lambda shape: pl.BlockSpec(shape, lambda b, c: (0,) * len(shape))
    return pl.pallas_call(
        _ssd_prompt_kernel,
        grid=(bsz, nc),
        in_specs=[pl.BlockSpec((CHUNK, D_INNER), lambda b, c: (row(b, c), 0)),
                  pl.BlockSpec((CHUNK, D_INNER), lambda b, c: (row(b, c), 1)),
                  pl.BlockSpec((CHUNK, BC_WIDTH), lambda b, c: (row(b, c), 4)),
                  pl.BlockSpec((CHUNK, LANES), lambda b, c: (row(b, c), 0)),
                  const((CONV_WIDTH, D_INNER)), const((CONV_WIDTH, BC_WIDTH)),
                  const((1, D_INNER)), const((1, BC_WIDTH)),
                  const((1, LANES)), const((1, LANES)),
                  const((1, D_INNER)), const((1, D_INNER)),
                  const((LANES, D_INNER)), const((CHUNK, CHUNK))],
        out_specs=[pl.BlockSpec((CHUNK, D_INNER), lambda b, c: (row(b, c), 0)),
                   pl.BlockSpec((1, SSM_HEADS, SSM_HEAD_DIM, SSM_STATE), lambda b, c: (b, 0, 0, 0)),
                   pl.BlockSpec((1, CONV_WIDTH - 1, CONV_DIM), lambda b, c: (b, 0, 0))],
        out_shape=[jax.ShapeDtypeStruct((m, D_INNER), BF16),
                   jax.ShapeDtypeStruct((bsz, SSM_HEADS, SSM_HEAD_DIM, SSM_STATE), F32),
                   jax.ShapeDtypeStruct((bsz, CONV_WIDTH - 1, CONV_DIM), F32)],
        scratch_shapes=[pltpu.VMEM((SSM_STATE, D_INNER), F32),
                        pltpu.VMEM((CHUNK + SUBLANES, D_INNER), F32),
                        pltpu.VMEM((CHUNK + SUBLANES, BC_WIDTH), F32)],
        compiler_params=_cparams(2),
    )(zx, zx, zx, dtp, prm["cwx"], prm["cwbc"], prm["cbx"], prm["cbbc"], prm["dtb"], prm["alog"],
      prm["dexp"], prm["nw"], prm["expand"], prm["tril"])


def _ssd_sample_kernel(z_ref, xr_ref, bcr_ref, dtr_ref, conv_ref, ssm_ref,
                       cwx_ref, cwbc_ref, cbx_ref, cbbc_ref, dtb_ref, alog_ref, dexp_ref, nw_ref,
                       e_ref, eye_ref,
                       y_ref, convo_ref, ssmo_ref, st_ref):
    prev = conv_ref[0]
    raw = jnp.concatenate([xr_ref[0], bcr_ref[0]], axis=1)
    cw = jnp.concatenate([cwx_ref[...], cwbc_ref[...]], axis=1)
    cbias = jnp.concatenate([cbx_ref[...], cbbc_ref[...]], axis=1)
    acc = cbias + raw * cw[CONV_WIDTH - 1:CONV_WIDTH, :]
    for k in range(CONV_WIDTH - 1):
        acc = acc + prev[k:k + 1, :] * cw[k:k + 1, :]
    convo_ref[0, 0:CONV_WIDTH - 2, :] = prev[1:CONV_WIDTH - 1, :]
    convo_ref[0, CONV_WIDTH - 2:CONV_WIDTH - 1, :] = raw
    xbc = _silu(acc)
    xs = xbc[:, 0:D_INNER]
    bm = xbc[:, D_INNER:D_INNER + SSM_GROUPS * SSM_STATE]
    cm = xbc[:, D_INNER + SSM_GROUPS * SSM_STATE:]

    dt = _softplus(dtr_ref[0] + dtb_ref[...])
    dec = jnp.exp(dt * -jnp.exp(alog_ref[...]))
    stacked = jnp.concatenate([dt, dec, jnp.zeros((SUBLANES - 2, LANES), F32)], axis=0)
    ex = _exact_dot(stacked, e_ref[...])
    dt_e = ex[0:1]
    dec_e = ex[1:2]
    xdt = xs * dt_e

    rows = jnp.concatenate([bm[:, g * SSM_STATE:(g + 1) * SSM_STATE] for g in range(SSM_GROUPS)]
                           + [cm[:, g * SSM_STATE:(g + 1) * SSM_STATE] for g in range(SSM_GROUPS)], axis=0)
    hi, mid, lo = _split3(rows)
    eye = eye_ref[...]
    cols = _dot_nt(eye, hi) + _dot_nt(eye, mid) + _dot_nt(eye, lo)

    for j in range(D_INNER // LANES):
        tile = ssm_ref[0, 2 * j:2 * j + 2].reshape(LANES, SSM_STATE)
        st_ref[:, j * LANES:(j + 1) * LANES] = tile.T
    y_groups = []
    for g in range(SSM_GROUPS):
        g0 = g * GROUP_WIDTH
        new = (st_ref[:, g0:g0 + GROUP_WIDTH] * dec_e[:, g0:g0 + GROUP_WIDTH]
               + cols[:, g:g + 1] * xdt[:, g0:g0 + GROUP_WIDTH])
        st_ref[:, g0:g0 + GROUP_WIDTH] = new
        y_groups.append(jnp.sum(new * cols[:, SSM_GROUPS + g:SSM_GROUPS + g + 1], axis=0, keepdims=True))
    for j in range(D_INNER // LANES):
        tile = st_ref[:, j * LANES:(j + 1) * LANES].T
        ssmo_ref[0, 2 * j:2 * j + 2] = tile.reshape(2, SSM_HEAD_DIM, SSM_STATE)
    y = jnp.concatenate(y_groups, axis=1) + xs * dexp_ref[...]

    hz = y * _silu(z_ref[0])
    normed = []
    for g in range(SSM_GROUPS):
        hg = hz[:, g * GROUP_WIDTH:(g + 1) * GROUP_WIDTH]
        normed.append(hg * lax.rsqrt(jnp.mean(hg * hg, axis=-1, keepdims=True) + RMS_EPS))
    y_ref[0] = jnp.concatenate(normed, axis=1) * nw_ref[...]


def _ssd_sample(zx, dts, conv_state, ssm_state, prm):
    nb = zx.shape[0]
    zx3 = zx.reshape(nb, 1, zx.shape[1])
    dt3 = dts.reshape(nb, 1, LANES)
    const = lambda shape: pl.BlockSpec(shape, lambda b: (0,) * len(shape))
    y, conv_new, ssm_new = pl.pallas_call(
        _ssd_sample_kernel,
        grid=(nb,),
        in_specs=[pl.BlockSpec((1, 1, D_INNER), lambda b: (b, 0, 0)),
                  pl.BlockSpec((1, 1, D_INNER), lambda b: (b, 0, 1)),
                  pl.BlockSpec((1, 1, BC_WIDTH), lambda b: (b, 0, 4)),
                  pl.BlockSpec((1, 1, LANES), lambda b: (b, 0, 0)),
                  pl.BlockSpec((1, CONV_WIDTH - 1, CONV_DIM), lambda b: (b, 0, 0)),
                  pl.BlockSpec((1, SSM_HEADS, SSM_HEAD_DIM, SSM_STATE), lambda b: (b, 0, 0, 0)),
                  const((CONV_WIDTH, D_INNER)), const((CONV_WIDTH, BC_WIDTH)),
                  const((1, D_INNER)), const((1, BC_WIDTH)),
                  const((1, LANES)), const((1, LANES)),
                  const((1, D_INNER)), const((1, D_INNER)),
                  const((LANES, D_INNER)), const((LANES, LANES))],
        out_specs=[pl.BlockSpec((1, 1, D_INNER), lambda b: (b, 0, 0)),
                   pl.BlockSpec((1, CONV_WIDTH - 1, CONV_DIM), lambda b: (b, 0, 0)),
                   pl.BlockSpec((1, SSM_HEADS, SSM_HEAD_DIM, SSM_STATE), lambda b: (b, 0, 0, 0))],
        out_shape=[jax.ShapeDtypeStruct((nb, 1, D_INNER), F32),
                   jax.ShapeDtypeStruct(conv_state.shape, F32),
                   jax.ShapeDtypeStruct(ssm_state.shape, F32)],
        scratch_shapes=[pltpu.VMEM((SSM_STATE, D_INNER), F32)],
        compiler_params=_cparams(1),
    )(zx3, zx3, zx3, dt3, conv_state, ssm_state, prm["cwx"], prm["cwbc"], prm["cbx"], prm["cbbc"],
      prm["dtb"], prm["alog"], prm["dexp"], prm["nw"], prm["expand"], prm["eye"])
    return y.reshape(nb, D_INNER), conv_new, ssm_new


def _attn_prompt_kernel(slope_ref, q_ref, kp_ref, kc_ref, vp_ref, vc_ref, o_ref, lse_ref, *, dil, has_prev):
    j = pl.program_id(2)
    qb = Q_BLOCK
    q = (q_ref[0] * ATT_SCALE).astype(BF16)
    row = lax.broadcasted_iota(jnp.int32, (qb, qb), 0)
    col = lax.broadcasted_iota(jnp.int32, (qb, qb), 1)
    step_c = row - col
    valid_c = step_c >= 0
    dist_c = (step_c * dil).astype(F32)
    if has_prev:
        valid_p = jnp.logical_and(col >= row, j > 0)
        dist_p = ((step_c + qb) * dil).astype(F32)
    lane = lax.broadcasted_iota(jnp.int32, (qb, LANES), 1)
    lane_lo = lane < ATT_HEAD_DIM
    lse_tile = jnp.zeros((qb, LANES), F32)
    for pr in range(ATT_HEADS // 2):
        sl = slice(pr * LANES, (pr + 1) * LANES)
        qp = q[:, sl]
        kc = kc_ref[0, :, sl].astype(BF16)
        vc = vc_ref[0, :, sl].astype(BF16)
        if has_prev:
            kp = kp_ref[0, :, sl].astype(BF16)
            vp = vp_ref[0, :, sl].astype(BF16)
        halves = []
        for half in range(2):
            h = 2 * pr + half
            slope = slope_ref[h]
            qh = jnp.where(lane_lo if half == 0 else jnp.logical_not(lane_lo), qp, jnp.zeros_like(qp))
            s_c = jnp.where(valid_c, _dot_nt(qh, kc) - slope * dist_c, -jnp.inf)
            mx = jnp.max(s_c, axis=-1, keepdims=True)
            if has_prev:
                s_p = jnp.where(valid_p, _dot_nt(qh, kp) - slope * dist_p, -jnp.inf)
                mx = jnp.maximum(mx, jnp.max(s_p, axis=-1, keepdims=True))
            p_c = jnp.exp(s_c - mx)
            den = jnp.sum(p_c, axis=-1, keepdims=True)
            acc = _dot(p_c.astype(BF16), vc)
            if has_prev:
                p_p = jnp.exp(s_p - mx)
                den = den + jnp.sum(p_p, axis=-1, keepdims=True)
                acc = acc + _dot(p_p.astype(BF16), vp)
            halves.append(acc / den)
            lse_tile = jnp.where(lane == h, mx + jnp.log(den), lse_tile)
        o_ref[0, :, sl] = jnp.where(lane_lo, halves[0], halves[1])
    lse_ref[0] = lse_tile


def _attn_prompt_group(proj, kvg, slopes, g, bsz, seq):
    win, dil = DIL_GROUPS[g]
    n = seq // dil
    assert win // dil == Q_BLOCK and n % Q_BLOCK == 0
    nblk = n // Q_BLOCK
    has_prev = nblk > 1
    proj_v = proj.reshape(bsz, n, dil * 4 * ATT_WIDTH)
    kv_v = kvg.reshape(bsz, n, dil * 2 * ATT_WIDTH)
    blk = (1, Q_BLOCK, ATT_WIDTH)
    prev = lambda j: jnp.maximum(j - 1, 0)
    o, lse = pl.pallas_call(
        functools.partial(_attn_prompt_kernel, dil=dil, has_prev=has_prev),
        grid=(bsz, dil, nblk),
        in_specs=[pl.BlockSpec(memory_space=pltpu.SMEM),
                  pl.BlockSpec(blk, lambda b, r, j: (b, j, r * 4 + g)),
                  pl.BlockSpec(blk, lambda b, r, j: (b, prev(j), r * 2)),
                  pl.BlockSpec(blk, lambda b, r, j: (b, j, r * 2)),
                  pl.BlockSpec(blk, lambda b, r, j: (b, prev(j), r * 2 + 1)),
                  pl.BlockSpec(blk, lambda b, r, j: (b, j, r * 2 + 1))],
        out_specs=[pl.BlockSpec(blk, lambda b, r, j: (b, j, r)),
                   pl.BlockSpec((1, Q_BLOCK, LANES), lambda b, r, j: (b, j, r))],
        out_shape=[jax.ShapeDtypeStruct((bsz, n, dil * ATT_WIDTH), F32),
                   jax.ShapeDtypeStruct((bsz, n, dil * LANES), F32)],
        compiler_params=_cparams(3),
    )(slopes, proj_v, kv_v, kv_v, kv_v, kv_v)
    return o.reshape(bsz * seq, ATT_WIDTH), lse.reshape(bsz * seq, LANES)


def _merge_out_kernel(o0_ref, o1_ref, o2_ref, l0_ref, l1_ref, l2_ref, gate_ref, e_ref, w_ref, r_ref,
                      g_ref, b_ref, out_ref):
    l0, l1, l2 = l0_ref[...], l1_ref[...], l2_ref[...]
    top = jnp.maximum(jnp.maximum(l0, l1), l2)
    w0, w1, w2 = jnp.exp(l0 - top), jnp.exp(l1 - top), jnp.exp(l2 - top)
    inv = 1.0 / (w0 + w1 + w2)
    e = e_ref[...]
    o = (_exact_dot(w0 * inv, e) * o0_ref[...] + _exact_dot(w1 * inv, e) * o1_ref[...]
         + _exact_dot(w2 * inv, e) * o2_ref[...])
    og = (o * _silu(gate_ref[...])).astype(BF16)
    v = DEEPNORM_ALPHA * r_ref[...] + _dot(og, w_ref[...])
    out_ref[...] = _layer_norm(v, g_ref[...], b_ref[...])


def _merge_out(os_, lses, proj, expand16, w, resid, g, b, *, tm=512):
    m = resid.shape[0]
    rowblk = lambda width: pl.BlockSpec((tm, width), lambda i: (i, 0))
    const = lambda shape: pl.BlockSpec(shape, lambda i: (0, 0))
    return pl.pallas_call(
        _merge_out_kernel,
        grid=(m // tm,),
        in_specs=[rowblk(ATT_WIDTH)] * 3 + [rowblk(LANES)] * 3
                 + [pl.BlockSpec((tm, ATT_WIDTH), lambda i: (i, 3)),
                    const((LANES, ATT_WIDTH)), const((ATT_WIDTH, D_MODEL)), rowblk(D_MODEL),
                    const((1, D_MODEL)), const((1, D_MODEL))],
        out_specs=rowblk(D_MODEL),
        out_shape=jax.ShapeDtypeStruct((m, D_MODEL), F32),
        compiler_params=_cparams(1),
    )(*os_, *lses, proj, expand16, w, resid, g, b)


def _attn_sample_kernel(slope_ref, q0_ref, q1_ref, q2_ref, gate_ref, n0_ref, n1_ref, n2_ref,
                        c0_ref, c1_ref, c2_ref, o_ref):
    nh = ATT_HEADS
    keys = Q_BLOCK
    lane = lax.broadcasted_iota(jnp.int32, (nh, ATT_WIDTH), 1)
    hrow = lax.broadcasted_iota(jnp.int32, (nh, ATT_WIDTH), 0)
    head_mask = (lane // ATT_HEAD_DIM) == hrow
    kidx = lax.broadcasted_iota(jnp.int32, (nh, 2 * keys), 1)
    newrow = lax.broadcasted_iota(jnp.int32, (keys, ATT_WIDTH), 0) == 0
    slope = slope_ref[:, 0:1]
    outs, lses = [], []
    for g, (q_ref, n_ref, c_ref) in enumerate(((q0_ref, n0_ref, c0_ref), (q1_ref, n1_ref, c1_ref),
                                               (q2_ref, n2_ref, c2_ref))):
        dil = DIL_GROUPS[g][1]
        q = q_ref[0] * ATT_SCALE
        qm = jnp.where(head_mask, jnp.broadcast_to(q, (nh, ATT_WIDTH)), 0.0).astype(BF16)
        new = n_ref[0]
        cache = c_ref[0]
        k_new = jnp.where(newrow, jnp.broadcast_to(new[:, 0:ATT_WIDTH], (keys, ATT_WIDTH)), 0.0)
        v_new = jnp.where(newrow, jnp.broadcast_to(new[:, ATT_WIDTH:], (keys, ATT_WIDTH)), 0.0)
        k_all = jnp.concatenate([cache[:, 0:ATT_WIDTH], k_new], axis=0).astype(BF16)
        v_all = jnp.concatenate([cache[:, ATT_WIDTH:], v_new], axis=0).astype(BF16)
        dist = (jnp.where(kidx < keys, keys - kidx, 0) * dil).astype(F32)
        s = jnp.where(kidx <= keys, _dot_nt(qm, k_all) - slope * dist, -jnp.inf)
        mx = jnp.max(s, axis=-1, keepdims=True)
        p = jnp.exp(s - mx)
        den = jnp.sum(p, axis=-1, keepdims=True)
        outs.append(_dot(p.astype(BF16), v_all) / den)
        lses.append(mx + jnp.log(den))
    top = jnp.maximum(jnp.maximum(lses[0], lses[1]), lses[2])
    ws = [jnp.exp(l - top) for l in lses]
    o = (ws[0] * outs[0] + ws[1] * outs[1] + ws[2] * outs[2]) / (ws[0] + ws[1] + ws[2])
    o = jnp.sum(jnp.where(head_mask, o, 0.0), axis=0, keepdims=True)
    o_ref[0] = o * _silu(gate_ref[0])


def _attn_sample(proj, kv_new, caches, slopes_b):
    nb = proj.shape[0]
    proj3 = proj.reshape(nb, 1, 4 * ATT_WIDTH)
    kvn3 = kv_new.reshape(nb, 1, N_DIL * 2 * ATT_WIDTH)
    views = []
    for (win, dil), cache in zip(DIL_GROUPS, caches):
        assert cache.shape[1] == win and win // dil == Q_BLOCK
        views.append(cache.reshape(nb, Q_BLOCK, dil * 2 * ATT_WIDTH))
    qspec = lambda g: pl.BlockSpec((1, 1, ATT_WIDTH), lambda b: (b, 0, g))
    nspec = lambda g: pl.BlockSpec((1, 1, 2 * ATT_WIDTH), lambda b: (b, 0, g))
    cspec = pl.BlockSpec((1, Q_BLOCK, 2 * ATT_WIDTH), lambda b: (b, 0, 0))
    o = pl.pallas_call(
        _attn_sample_kernel,
        grid=(nb,),
        in_specs=[pl.BlockSpec((ATT_HEADS, LANES), lambda b: (0, 0)),
                  qspec(0), qspec(1), qspec(2), qspec(3), nspec(0), nspec(1), nspec(2),
                  cspec, cspec, cspec],
        out_specs=pl.BlockSpec((1, 1, ATT_WIDTH), lambda b: (b, 0, 0)),
        out_shape=jax.ShapeDtypeStruct((nb, 1, ATT_WIDTH), F32),
        compiler_params=_cparams(1),
    )(slopes_b, proj3, proj3, proj3, proj3, kvn3, kvn3, kvn3, *views)
    return o.reshape(nb, ATT_WIDTH)


N_SHIFT_DMAS = 8


def _cache_shift_kernel(cache_ref, new_ref, out_ref, sems, *, row):
    keep = cache_ref.shape[1] - row
    per = keep // N_SHIFT_DMAS
    copies = []
    for i in range(N_SHIFT_DMAS):
        copies.append(pltpu.make_async_copy(cache_ref.at[:, pl.ds(row + i * per, per)],
                                            out_ref.at[:, pl.ds(i * per, per)], sems.at[i]))
    copies.append(pltpu.make_async_copy(new_ref, out_ref.at[:, pl.ds(keep, row)], sems.at[N_SHIFT_DMAS]))
    for cp in copies:
        cp.start()
    for cp in copies:
        cp.wait()


def _cache_shift(cache, new_rows):
    nb, win = cache.shape[0], cache.shape[1]
    row = 2 * ATT_WIDTH
    assert ((win - 1) * row) % (N_SHIFT_DMAS * LANES) == 0 and new_rows.shape == (nb, row)
    out = pl.pallas_call(
        functools.partial(_cache_shift_kernel, row=row),
        in_specs=[pl.BlockSpec(memory_space=pl.ANY), pl.BlockSpec(memory_space=pl.ANY)],
        out_specs=pl.BlockSpec(memory_space=pl.ANY),
        out_shape=jax.ShapeDtypeStruct((nb, win * row), cache.dtype),
        scratch_shapes=[pltpu.SemaphoreType.DMA((N_SHIFT_DMAS + 1,))],
    )(cache.reshape(nb, win * row), new_rows)
    return out.reshape(cache.shape)


def _pad_lanes(v):
    return jnp.pad(v.astype(F32), (0, LANES - v.shape[0])).reshape(1, LANES)


def kernel(x_prompt, x_sample, state_ssm, state_conv, cache_kv_w128, cache_kv_w512, cache_kv_w2048,
           a_in_proj, a_conv_w, a_conv_b, a_dt_bias, a_log, a_d, a_norm_w, a_out_proj,
           kv_proj, b_in_proj, b_out_proj, ln_g, ln_b):
    bsz, seq, _ = x_prompt.shape
    nb = x_sample.shape[0]
    assert x_sample.shape[1] == 1 and seq % CHUNK == 0
    caches = (cache_kv_w128, cache_kv_w512, cache_kv_w2048)

    heads = jnp.arange(LANES)[:, None]
    expand32 = (heads == jnp.arange(D_INNER)[None, :] // SSM_HEAD_DIM).astype(BF16)
    expand16 = (heads == jnp.arange(ATT_WIDTH)[None, :] // ATT_HEAD_DIM).astype(BF16)
    tril = (jnp.arange(CHUNK)[:, None] >= jnp.arange(CHUNK)[None, :]).astype(BF16)
    eye = jnp.eye(LANES, dtype=BF16)
    slopes = jnp.exp2(-8.0 * jnp.arange(1, ATT_HEADS + 1, dtype=F32) / ATT_HEADS)
    slopes_b = jnp.broadcast_to(slopes[:, None], (ATT_HEADS, LANES))

    hp = x_prompt.reshape(bsz * seq, D_MODEL)
    hs = x_sample.reshape(nb, D_MODEL)
    ssm_p, conv_p, ssm_s, conv_s = [], [], [], []

    for i in range(N_A_LAYERS):
        w_in = a_in_proj[i]
        w_main = w_in[:, 0:D_INNER + CONV_DIM].astype(BF16)
        w_dt = jnp.pad(w_in[:, D_INNER + CONV_DIM:], ((0, 0), (0, LANES - SSM_HEADS))).astype(BF16)
        w_out = a_out_proj[i].astype(BF16)
        prm = dict(
            cwx=a_conv_w[i][:, 0:D_INNER], cwbc=a_conv_w[i][:, D_INNER:],
            cbx=a_conv_b[i][0:D_INNER].reshape(1, D_INNER), cbbc=a_conv_b[i][D_INNER:].reshape(1, BC_WIDTH),
            dtb=_pad_lanes(a_dt_bias[i]), alog=_pad_lanes(a_log[i]),
            dexp=jnp.repeat(a_d[i].astype(F32), SSM_HEAD_DIM).reshape(1, D_INNER),
            nw=a_norm_w[i].reshape(1, D_INNER), expand=expand32, tril=tril, eye=eye)
        g_ln, b_ln = ln_g[i].reshape(1, D_MODEL), ln_b[i].reshape(1, D_MODEL)

        zx = _matmul(hp, w_main, tn=512)
        dtp = _matmul(hp, w_dt, tn=LANES)
        yn, ssm_new, conv_new = _ssd_prompt(zx, dtp, prm, bsz, seq)
        hp = _matmul_ln(yn, w_out, hp, g_ln, b_ln)
        ssm_p.append(ssm_new)
        conv_p.append(conv_new)

        zx_s = _matmul(hs, w_main, tn=512)
        dt_s = _matmul(hs, w_dt, tn=LANES)
        yn_s, conv_new_s, ssm_new_s = _ssd_sample(zx_s, dt_s, state_conv[i], state_ssm[i], prm)
        hs = _matmul_ln(yn_s, w_out, hs, g_ln, b_ln)
        ssm_s.append(ssm_new_s)
        conv_s.append(conv_new_s)

    kvw = kv_proj.reshape(D_MODEL, 2, N_DIL, ATT_WIDTH)
    kvw_g = [jnp.concatenate([kvw[:, 0, g], kvw[:, 1, g]], axis=1).astype(BF16) for g in range(N_DIL)]
    kv_p = [_matmul(hp, kvw_g[g], tn=1024) for g in range(N_DIL)]
    kv_s = _matmul(hs, jnp.concatenate(kvw_g, axis=1), tn=1024)
    new_kv_p = []
    for g, (win, _) in enumerate(DIL_GROUPS):
        rows = kv_p[g].reshape(bsz, seq, 2, ATT_HEADS, ATT_HEAD_DIM)
        new_kv_p.append(rows[:, seq - min(win, seq):])
    new_kv_s = [_cache_shift(caches[g], kv_s[:, g * 2 * ATT_WIDTH:(g + 1) * 2 * ATT_WIDTH])
                for g in range(N_DIL)]

    for j in range(N_B_LAYERS):
        layer = N_A_LAYERS + j
        w_in = b_in_proj[j].astype(BF16)
        w_out = b_out_proj[j].astype(BF16)
        g_ln, b_ln = ln_g[layer].reshape(1, D_MODEL), ln_b[layer].reshape(1, D_MODEL)

        proj = _matmul(hp, w_in, tn=1024)
        res = [_attn_prompt_group(proj, kv_p[g], slopes, g, bsz, seq) for g in range(N_DIL)]
        hp = _merge_out([r[0] for r in res], [r[1] for r in res], proj, expand16, w_out, hp, g_ln, b_ln)

        proj_s = _matmul(hs, w_in, tn=1024)
        og_s = _attn_sample(proj_s, kv_s, caches, slopes_b)
        hs = _matmul_ln(og_s, w_out, hs, g_ln, b_ln)

    return (hp.reshape(bsz, seq, D_MODEL), hs.reshape(nb, 1, D_MODEL),
            jnp.stack(ssm_p), jnp.stack(conv_p), new_kv_p[0], new_kv_p[1], new_kv_p[2],
            jnp.stack(ssm_s), jnp.stack(conv_s), new_kv_s[0], new_kv_s[1], new_kv_s[2])
```

```python
import functools

import jax
import jax.numpy as jnp
from jax import lax
from jax.experimental import pallas as pl
from jax.experimental.pallas import tpu as pltpu

F32 = jnp.float32
BF16 = jnp.bfloat16

D_MODEL = 1024
N_A_LAYERS = 2
N_B_LAYERS = 2
D_INNER = 2048
SSM_HEAD_DIM = 64
SSM_HEADS = 32
SSM_GROUPS = 4
SSM_STATE = 128
HEADS_PER_GROUP = SSM_HEADS // SSM_GROUPS
GROUP_WIDTH = D_INNER // SSM_GROUPS
CONV_WIDTH = 4
BC_WIDTH = 2 * SSM_GROUPS * SSM_STATE
CONV_DIM = D_INNER + BC_WIDTH
CHUNK = 128
DIL_GROUPS = ((128, 1), (512, 4), (2048, 16))
N_DIL = 3
ATT_HEADS = 16
ATT_HEAD_DIM = 64
ATT_WIDTH = 1024
KV_ROW = 2 * ATT_WIDTH
Q_BLOCK = 128
ATT_SCALE = ATT_HEAD_DIM ** -0.5
LN_EPS = 1e-5
RMS_EPS = 1e-5
DEEPNORM_ALPHA = (2.0 * 4) ** 0.25

LANES = 128
SUBLANES = 8
VMEM_LIMIT = 48 * 1024 * 1024

ATT_HEAD_BATCH = 4
PROJ_ROWS = 512
KV_ROWS = 256
SHIFT_BLOCK_BYTES = 4 * 1024 * 1024


def _cparams(n_grid):
    return pltpu.CompilerParams(dimension_semantics=("arbitrary",) * n_grid, vmem_limit_bytes=VMEM_LIMIT)


def _silu(x):
    return x * (1.0 / (1.0 + jnp.exp(-x)))


def _softplus(x):
    return jnp.maximum(x, 0.0) + jnp.log1p(jnp.exp(-jnp.abs(x)))


def _split3(a):
    hi = a.astype(BF16)
    r1 = a - hi.astype(F32)
    mid = r1.astype(BF16)
    lo = (r1 - mid.astype(F32)).astype(BF16)
    return hi, mid, lo


def _dot(a, b):
    return jnp.dot(a, b, preferred_element_type=F32)


def _dot_nt(a, b):
    return lax.dot_general(a, b, (((1,), (1,)), ((), ())), preferred_element_type=F32)


def _exact_dot(a, m01):
    hi, mid, lo = _split3(a)
    return _dot(hi, m01) + _dot(mid, m01) + _dot(lo, m01)


def _exact_dot_left(m01, a):
    hi, mid, lo = _split3(a)
    return _dot(m01, hi) + _dot(m01, mid) + _dot(m01, lo)


def _layer_norm(v, g, b):
    mu = jnp.mean(v, axis=-1, keepdims=True)
    d = v - mu
    var = jnp.mean(d * d, axis=-1, keepdims=True)
    return d * lax.rsqrt(var + LN_EPS) * g + b


def _lane_blocks(width):
    return [slice(cb * LANES, (cb + 1) * LANES) for cb in range(width // LANES)]


def _store_lane_blocked(blk_ref, val):
    for cb, sl in enumerate(_lane_blocks(val.shape[1])):
        blk_ref[cb] = val[:, sl]


def _load_lane_blocked(blk_ref):
    return jnp.concatenate([blk_ref[cb] for cb in range(blk_ref.shape[0])], axis=1)


def _deinterleave(dst_ref, blk_ref, dil):
    rows = blk_ref.shape[1] // dil
    for r in range(dil):
        for cb, sl in enumerate(_lane_blocks(dst_ref.shape[-1])):
            dst_ref[0, r, :, sl] = blk_ref[cb, pl.ds(r, rows, stride=dil), :]


def _interleave(blk_ref, src_ref, dil):
    rows = blk_ref.shape[1] // dil
    for r in range(dil):
        for cb, sl in enumerate(_lane_blocks(src_ref.shape[-1])):
            blk_ref[cb, pl.ds(r, rows, stride=dil), :] = src_ref[0, r, :, sl]


def _mm_kernel(x_ref, w_ref, o_ref, xb_ref):
    @pl.when(pl.program_id(1) == 0)
    def _():
        xb_ref[...] = x_ref[...].astype(BF16)

    o_ref[...] = _dot(xb_ref[...], w_ref[...]).astype(o_ref.dtype)


def _matmul(x, w, *, tn, out_dtype=F32, tm=1024):
    m, k = x.shape
    n = w.shape[1]
    tm = min(tm, m)
    assert m % tm == 0 and n % tn == 0
    return pl.pallas_call(
        _mm_kernel,
        grid=(m // tm, n // tn),
        in_specs=[pl.BlockSpec((tm, k), lambda i, j: (i, 0)),
                  pl.BlockSpec((k, tn), lambda i, j: (0, j))],
        out_specs=pl.BlockSpec((tm, tn), lambda i, j: (i, j)),
        out_shape=jax.ShapeDtypeStruct((m, n), out_dtype),
        scratch_shapes=[pltpu.VMEM((tm, k), BF16)],
        compiler_params=_cparams(2),
    )(x, w)


def _mm_nt_kernel(w_ref, x_ref, o_ref):
    o_ref[...] = _dot_nt(w_ref[...], x_ref[...].astype(BF16))


def _matmul_nt(w_t, x, *, tn):
    n, k = w_t.shape
    m = x.shape[0]
    assert n % tn == 0
    return pl.pallas_call(
        _mm_nt_kernel,
        grid=(n // tn,),
        in_specs=[pl.BlockSpec((tn, k), lambda i: (i, 0)),
                  pl.BlockSpec((m, k), lambda i: (0, 0))],
        out_specs=pl.BlockSpec((tn, m), lambda i: (i, 0)),
        out_shape=jax.ShapeDtypeStruct((n, m), F32),
        compiler_params=_cparams(1),
    )(w_t, x)


def _mm_ln_kernel(y_ref, w_ref, r_ref, g_ref, b_ref, o_ref):
    acc = _dot(y_ref[...].astype(BF16), w_ref[...])
    v = DEEPNORM_ALPHA * r_ref[...] + acc
    o_ref[...] = _layer_norm(v, g_ref[...], b_ref[...])


def _matmul_ln(y, w, resid, g, b, *, tm=512):
    m, k = y.shape
    n = w.shape[1]
    tm = min(tm, m)
    assert m % tm == 0
    return pl.pallas_call(
        _mm_ln_kernel,
        grid=(m // tm,),
        in_specs=[pl.BlockSpec((tm, k), lambda i: (i, 0)),
                  pl.BlockSpec((k, n), lambda i: (0, 0)),
                  pl.BlockSpec((tm, n), lambda i: (i, 0)),
                  pl.BlockSpec((1, n), lambda i: (0, 0)),
                  pl.BlockSpec((1, n), lambda i: (0, 0))],
        out_specs=pl.BlockSpec((tm, n), lambda i: (i, 0)),
        out_shape=jax.ShapeDtypeStruct((m, n), F32),
        compiler_params=_cparams(1),
    )(y, w, resid, g, b)


def _ssd_prompt_kernel(z_ref, xr_ref, bcr_ref, dtr_ref, cwx_ref, cwbc_ref, cbx_ref, cbbc_ref,
                       dtb_ref, alog_ref, dexp_ref, nw_ref, e_ref, tril_ref,
                       y_ref, ssm_ref, conv_ref,
                       st_ref, extx_ref, extbc_ref):
    c = pl.program_id(1)
    t = CHUNK

    @pl.when(c == 0)
    def _():
        st_ref[...] = jnp.zeros_like(st_ref)
        extx_ref[0:SUBLANES, :] = jnp.zeros((SUBLANES, D_INNER), F32)
        extbc_ref[0:SUBLANES, :] = jnp.zeros((SUBLANES, BC_WIDTH), F32)

    extx_ref[SUBLANES:SUBLANES + t, :] = xr_ref[...]
    extbc_ref[SUBLANES:SUBLANES + t, :] = bcr_ref[...]

    def conv(ext_ref, w_ref, b_ref):
        acc = b_ref[...]
        for k in range(CONV_WIDTH):
            off = SUBLANES - (CONV_WIDTH - 1) + k
            acc = acc + ext_ref[pl.ds(off, t), :] * w_ref[k:k + 1, :]
        return acc

    xs = _silu(conv(extx_ref, cwx_ref, cbx_ref))
    bc = _silu(conv(extbc_ref, cwbc_ref, cbbc_ref))
    tail = t + SUBLANES - (CONV_WIDTH - 1)
    conv_ref[0, :, 0:D_INNER] = extx_ref[pl.ds(tail, CONV_WIDTH - 1), :]
    conv_ref[0, :, D_INNER:CONV_DIM] = extbc_ref[pl.ds(tail, CONV_WIDTH - 1), :]
    extx_ref[0:SUBLANES, :] = extx_ref[t:t + SUBLANES, :]
    extbc_ref[0:SUBLANES, :] = extbc_ref[t:t + SUBLANES, :]

    bm = bc[:, 0:SSM_GROUPS * SSM_STATE]
    cm = bc[:, SSM_GROUPS * SSM_STATE:]

    dt = _softplus(dtr_ref[...] + dtb_ref[...])
    a = -jnp.exp(alog_ref[...])
    acs = _exact_dot_left(tril_ref[...], dt * a)
    acs_t = acs.T
    a_last = acs[t - 1:t, :]
    stacked = jnp.concatenate(
        [dt, jnp.exp(acs), jnp.exp(a_last - acs), jnp.broadcast_to(jnp.exp(a_last), (SUBLANES, LANES))], axis=0)
    ex = _exact_dot(stacked, e_ref[...])
    dt_e = ex[0:t]
    ea_e = ex[t:2 * t]
    te_e = ex[2 * t:3 * t]
    cd_e = ex[3 * t:3 * t + 1]

    xdt = xs * dt_e
    xdt_b = xdt.astype(BF16)
    xw_b = (xdt * te_e).astype(BF16)

    row = lax.broadcasted_iota(jnp.int32, (t, t), 0)
    col = lax.broadcasted_iota(jnp.int32, (t, t), 1)
    causal = row >= col
    lane_lo = lax.broadcasted_iota(jnp.int32, (t, LANES), 1) < SSM_HEAD_DIM

    y_groups = []
    for g in range(SSM_GROUPS):
        g0 = g * GROUP_WIDTH
        cg_b = cm[:, g * SSM_STATE:(g + 1) * SSM_STATE].astype(BF16)
        bg = bm[:, g * SSM_STATE:(g + 1) * SSM_STATE]
        cb = _dot_nt(cg_b, bg.astype(BF16))
        s_old = st_ref[:, g0:g0 + GROUP_WIDTH]
        y_off = _dot(cg_b, s_old.astype(BF16)) * ea_e[:, g0:g0 + GROUP_WIDTH]
        states = _dot(bg.T.astype(BF16), xw_b[:, g0:g0 + GROUP_WIDTH])
        st_ref[:, g0:g0 + GROUP_WIDTH] = s_old * cd_e[:, g0:g0 + GROUP_WIDTH] + states
        pairs = []
        for pr in range(HEADS_PER_GROUP // 2):
            xp = xdt_b[:, g0 + pr * LANES:g0 + (pr + 1) * LANES]
            halves = []
            for half in range(2):
                h = g * HEADS_PER_GROUP + 2 * pr + half
                seg = acs[:, h:h + 1] - acs_t[h:h + 1, :]
                dec = jnp.exp(jnp.where(causal, seg, -jnp.inf))
                halves.append(_dot((cb * dec).astype(BF16), xp))
            pairs.append(jnp.where(lane_lo, halves[0], halves[1]))
        y_groups.append(jnp.concatenate(pairs, axis=1) + y_off)
    y = jnp.concatenate(y_groups, axis=1) + xs * dexp_ref[...]

    hz = y * _silu(z_ref[...])
    normed = []
    for g in range(SSM_GROUPS):
        hg = hz[:, g * GROUP_WIDTH:(g + 1) * GROUP_WIDTH]
        normed.append(hg * lax.rsqrt(jnp.mean(hg * hg, axis=-1, keepdims=True) + RMS_EPS))
    y_ref[...] = (jnp.concatenate(normed, axis=1) * nw_ref[...]).astype(y_ref.dtype)

    @pl.when(c == pl.num_programs(1) - 1)
    def _():
        for j in range(D_INNER // LANES):
            tile = st_ref[:, j * LANES:(j + 1) * LANES].T
            ssm_ref[0, 2 * j:2 * j + 2] = tile.reshape(2, SSM_HEAD_DIM, SSM_STATE)


def _ssd_prompt(zx, dtp, prm, bsz, seq):
    nc = seq // CHUNK
    m = bsz * seq
    row = lambda b, c: b * nc + c
    const = lambda shape: pl.BlockSpec(shape, lambda b, c: (0,) * len(shape))
    return pl.pallas_call(
        _ssd_prompt_kernel,
        grid=(bsz, nc),
        in_specs=[pl.BlockSpec((CHUNK, D_INNER), lambda b, c: (row(b, c), 0)),
                  pl.BlockSpec((CHUNK, D_INNER), lambda b, c: (row(b, c), 1)),
                  pl.BlockSpec((CHUNK, BC_WIDTH), lambda b, c: (row(b, c), 4)),
                  pl.BlockSpec((CHUNK, LANES), lambda b, c: (row(b, c), 0)),
                  const((CONV_WIDTH, D_INNER)), const((CONV_WIDTH, BC_WIDTH)),
                  const((1, D_INNER)), const((1, BC_WIDTH)),
                  const((1, LANES)), const((1, LANES)),
                  const((1, D_INNER)), const((1, D_INNER)),
                  const((LANES, D_INNER)), const((CHUNK, CHUNK))],
        out_specs=[pl.BlockSpec((CHUNK, D_INNER), lambda b, c: (row(b, c), 0)),
                   pl.BlockSpec((1, SSM_HEADS, SSM_HEAD_DIM, SSM_STATE), lambda b, c: (b, 0, 0, 0)),
                   pl.BlockSpec((1, CONV_WIDTH - 1, CONV_DIM), lambda b, c: (b, 0, 0))],
        out_shape=[jax.ShapeDtypeStruct((m, D_INNER), BF16),
                   jax.ShapeDtypeStruct((bsz, SSM_HEADS, SSM_HEAD_DIM, SSM_STATE), F32),
                   jax.ShapeDtypeStruct((bsz, CONV_WIDTH - 1, CONV_DIM), F32)],
        scratch_shapes=[pltpu.VMEM((SSM_STATE, D_INNER), F32),
                        pltpu.VMEM((CHUNK + SUBLANES, D_INNER), F32),
                        pltpu.VMEM((CHUNK + SUBLANES, BC_WIDTH), F32)],
        compiler_params=_cparams(2),
    )(zx, zx, zx, dtp, prm["cwx"], prm["cwbc"], prm["cbx"], prm["cbbc"], prm["dtb"], prm["alog"],
      prm["dexp"], prm["nw"], prm["expand"], prm["tril"])


def _ssd_sample_kernel(z_ref, xr_ref, bcr_ref, dtr_ref, conv_ref, ssm_ref,
                       cwx_ref, cwbc_ref, cbx_ref, cbbc_ref, dtb_ref, alog_ref, dexp_ref, nw_ref,
                       e_ref, eye_ref,
                       y_ref, convo_ref, ssmo_ref, st_ref):
    prev = conv_ref[0]
    raw = jnp.concatenate([xr_ref[0], bcr_ref[0]], axis=1)
    cw = jnp.concatenate([cwx_ref[...], cwbc_ref[...]], axis=1)
    cbias = jnp.concatenate([cbx_ref[...], cbbc_ref[...]], axis=1)
    acc = cbias + raw * cw[CONV_WIDTH - 1:CONV_WIDTH, :]
    for k in range(CONV_WIDTH - 1):
        acc = acc + prev[k:k + 1, :] * cw[k:k + 1, :]
    convo_ref[0, 0:CONV_WIDTH - 2, :] = prev[1:CONV_WIDTH - 1, :]
    convo_ref[0, CONV_WIDTH - 2:CONV_WIDTH - 1, :] = raw
    xbc = _silu(acc)
    xs = xbc[:, 0:D_INNER]
    bm = xbc[:, D_INNER:D_INNER + SSM_GROUPS * SSM_STATE]
    cm = xbc[:, D_INNER + SSM_GROUPS * SSM_STATE:]

    dt = _softplus(dtr_ref[0] + dtb_ref[...])
    dec = jnp.exp(dt * -jnp.exp(alog_ref[...]))
    stacked = jnp.concatenate([dt, dec, jnp.zeros((SUBLANES - 2, LANES), F32)], axis=0)
    ex = _exact_dot(stacked, e_ref[...])
    dt_e = ex[0:1]
    dec_e = ex[1:2]
    xdt = xs * dt_e

    rows = jnp.concatenate([bm[:, g * SSM_STATE:(g + 1) * SSM_STATE] for g in range(SSM_GROUPS)]
                           + [cm[:, g * SSM_STATE:(g + 1) * SSM_STATE] for g in range(SSM_GROUPS)], axis=0)
    hi, mid, lo = _split3(rows)
    eye = eye_ref[...]
    cols = _dot_nt(eye, hi) + _dot_nt(eye, mid) + _dot_nt(eye, lo)

    for j in range(D_INNER // LANES):
        tile = ssm_ref[0, 2 * j:2 * j + 2].reshape(LANES, SSM_STATE)
        st_ref[:, j * LANES:(j + 1) * LANES] = tile.T
    y_groups = []
    for g in range(SSM_GROUPS):
        g0 = g * GROUP_WIDTH
        new = (st_ref[:, g0:g0 + GROUP_WIDTH] * dec_e[:, g0:g0 + GROUP_WIDTH]
               + cols[:, g:g + 1] * xdt[:, g0:g0 + GROUP_WIDTH])
        st_ref[:, g0:g0 + GROUP_WIDTH] = new
        y_groups.append(jnp.sum(new * cols[:, SSM_GROUPS + g:SSM_GROUPS + g + 1], axis=0, keepdims=True))
    for j in range(D_INNER // LANES):
        tile = st_ref[:, j * LANES:(j + 1) * LANES].T
        ssmo_ref[0, 2 * j:2 * j + 2] = tile.reshape(2, SSM_HEAD_DIM, SSM_STATE)
    y = jnp.concatenate(y_groups, axis=1) + xs * dexp_ref[...]

    hz = y * _silu(z_ref[0])
    normed = []
    for g in range(SSM_GROUPS):
        hg = hz[:, g * GROUP_WIDTH:(g + 1) * GROUP_WIDTH]
        normed.append(hg * lax.rsqrt(jnp.mean(hg * hg, axis=-1, keepdims=True) + RMS_EPS))
    y_ref[0] = jnp.concatenate(normed, axis=1) * nw_ref[...]


def _ssd_sample(zx, dts, conv_state, ssm_state, prm):
    nb = zx.shape[0]
    zx3 = zx.reshape(nb, 1, zx.shape[1])
    dt3 = dts.reshape(nb, 1, LANES)
    const = lambda shape: pl.BlockSpec(shape, lambda b: (0,) * len(shape))
    y, conv_new, ssm_new = pl.pallas_call(
        _ssd_sample_kernel,
        grid=(nb,),
        in_specs=[pl.BlockSpec((1, 1, D_INNER), lambda b: (b, 0, 0)),
                  pl.BlockSpec((1, 1, D_INNER), lambda b: (b, 0, 1)),
                  pl.BlockSpec((1, 1, BC_WIDTH), lambda b: (b, 0, 4)),
                  pl.BlockSpec((1, 1, LANES), lambda b: (b, 0, 0)),
                  pl.BlockSpec((1, CONV_WIDTH - 1, CONV_DIM), lambda b: (b, 0, 0)),
                  pl.BlockSpec((1, SSM_HEADS, SSM_HEAD_DIM, SSM_STATE), lambda b: (b, 0, 0, 0)),
                  const((CONV_WIDTH, D_INNER)), const((CONV_WIDTH, BC_WIDTH)),
                  const((1, D_INNER)), const((1, BC_WIDTH)),
                  const((1, LANES)), const((1, LANES)),
                  const((1, D_INNER)), const((1, D_INNER)),
                  const((LANES, D_INNER)), const((LANES, LANES))],
        out_specs=[pl.BlockSpec((1, 1, D_INNER), lambda b: (b, 0, 0)),
                   pl.BlockSpec((1, CONV_WIDTH - 1, CONV_DIM), lambda b: (b, 0, 0)),
                   pl.BlockSpec((1, SSM_HEADS, SSM_HEAD_DIM, SSM_STATE), lambda b: (b, 0, 0, 0))],
        out_shape=[jax.ShapeDtypeStruct((nb, 1, D_INNER), F32),
                   jax.ShapeDtypeStruct(conv_state.shape, F32),
                   jax.ShapeDtypeStruct(ssm_state.shape, F32)],
        scratch_shapes=[pltpu.VMEM((SSM_STATE, D_INNER), F32)],
        compiler_params=_cparams(1),
    )(zx3, zx3, zx3, dt3, conv_state, ssm_state, prm["cwx"], prm["cwbc"], prm["cbx"], prm["cbbc"],
      prm["dtb"], prm["alog"], prm["dexp"], prm["nw"], prm["expand"], prm["eye"])
    return y.reshape(nb, D_INNER), conv_new, ssm_new


def _b_in_proj_kernel(x_ref, w_ref, q0_ref, q1_ref, q2_ref, gate_ref, xb_ref, acc_ref):
    j = pl.program_id(1)

    @pl.when(j == 0)
    def _():
        xb_ref[...] = x_ref[...].astype(BF16)

    acc = _dot(xb_ref[...], w_ref[...])
    for g, dst in enumerate((q0_ref, q1_ref, q2_ref)):
        @pl.when(j == g)
        def _(g=g, dst=dst):
            if DIL_GROUPS[g][1] == 1:
                dst[0, 0] = acc
            else:
                _store_lane_blocked(acc_ref, acc)
                _deinterleave(dst, acc_ref, DIL_GROUPS[g][1])

    @pl.when(j == N_DIL)
    def _():
        gate_ref[...] = acc


def _b_in_proj(x, w, bsz, seq):
    tm = PROJ_ROWS
    tpb = seq // tm
    m, k = x.shape
    qspecs, qshapes = [], []
    for _, dil in DIL_GROUPS:
        assert tm % (dil * SUBLANES) == 0
        qspecs.append(pl.BlockSpec((1, dil, tm // dil, ATT_WIDTH), lambda i, j: (i // tpb, 0, i % tpb, 0)))
        qshapes.append(jax.ShapeDtypeStruct((bsz, dil, seq // dil, ATT_WIDTH), F32))
    return pl.pallas_call(
        _b_in_proj_kernel,
        grid=(m // tm, N_DIL + 1),
        in_specs=[pl.BlockSpec((tm, k), lambda i, j: (i, 0)),
                  pl.BlockSpec((k, ATT_WIDTH), lambda i, j: (0, j))],
        out_specs=qspecs + [pl.BlockSpec((tm, ATT_WIDTH), lambda i, j: (i, 0))],
        out_shape=qshapes + [jax.ShapeDtypeStruct((m, ATT_WIDTH), F32)],
        scratch_shapes=[pltpu.VMEM((tm, k), BF16), pltpu.VMEM((ATT_WIDTH // LANES, tm, LANES), F32)],
        compiler_params=_cparams(2),
    )(x, w)


def _kv_proj_kernel(x_ref, w_ref, kv_ref, kvt_ref, acc_ref, *, dil, first_tile, wt):
    t = pl.program_id(1)
    acc = _dot(x_ref[...].astype(BF16), w_ref[...])
    _store_lane_blocked(acc_ref, acc)
    if dil == 1:
        kv_ref[0, 0] = acc
    else:
        _deinterleave(kv_ref, acc_ref, dil)

    @pl.when(t >= first_tile)
    def _():
        tm = acc_ref.shape[1]
        for cb in range(KV_ROW // LANES):
            for rb in range(wt // LANES):
                r0 = tm - wt + rb * LANES
                kvt_ref[0, cb * LANES:(cb + 1) * LANES, rb * LANES:(rb + 1) * LANES] = (
                    acc_ref[cb, r0:r0 + LANES, :].T)


def _kv_proj(x, w, g, bsz, seq):
    win, dil = DIL_GROUPS[g]
    win = min(win, seq)
    tm = KV_ROWS
    tpb = seq // tm
    wt = min(win, tm)
    first_tile = tpb - win // wt
    assert tm % (dil * SUBLANES) == 0 and win % wt == 0 and seq % tm == 0
    k = x.shape[1]
    return pl.pallas_call(
        functools.partial(_kv_proj_kernel, dil=dil, first_tile=first_tile, wt=wt),
        grid=(bsz, tpb),
        in_specs=[pl.BlockSpec((tm, k), lambda b, t: (b * tpb + t, 0)),
                  pl.BlockSpec((k, KV_ROW), lambda b, t: (0, 0))],
        out_specs=[pl.BlockSpec((1, dil, tm // dil, KV_ROW), lambda b, t: (b, 0, t, 0)),
                   pl.BlockSpec((1, KV_ROW, wt), lambda b, t: (b, 0, jnp.maximum(t - first_tile, 0)))],
        out_shape=[jax.ShapeDtypeStruct((bsz, dil, seq // dil, KV_ROW), F32),
                   jax.ShapeDtypeStruct((bsz, KV_ROW, win), F32)],
        scratch_shapes=[pltpu.VMEM((KV_ROW // LANES, tm, LANES), F32)],
        compiler_params=_cparams(2),
    )(x, w)


def _attn_prompt_kernel(slope_ref, q_ref, kvc_ref, *rest, dil, has_prev):
    kvp_ref, o_ref, lse_ref = rest if has_prev else (None,) + rest
    j = pl.program_id(2)
    qb = Q_BLOCK
    q = (q_ref[0, 0] * ATT_SCALE).astype(BF16)
    row = lax.broadcasted_iota(jnp.int32, (qb, qb), 0)
    col = lax.broadcasted_iota(jnp.int32, (qb, qb), 1)
    lower = col <= row
    diag = col == row
    dist = (((row - col) & (qb - 1)) * dil).astype(F32)
    far = float(qb * dil)
    lane = lax.broadcasted_iota(jnp.int32, (qb, LANES), 1)
    lane_lo = lane < ATT_HEAD_DIM
    if has_prev:
        live = jnp.logical_or(lower, j > 0)
    lse_tile = jnp.zeros((qb, LANES), F32)

    for hb in range(0, ATT_HEADS, ATT_HEAD_BATCH):
        heads = list(range(hb, hb + ATT_HEAD_BATCH))
        kc, vc, kp, vp = {}, {}, {}, {}
        for pr in sorted({h // 2 for h in heads}):
            sl = slice(pr * LANES, (pr + 1) * LANES)
            vsl = slice(ATT_WIDTH + pr * LANES, ATT_WIDTH + (pr + 1) * LANES)
            kc[pr] = kvc_ref[0, 0, :, sl].astype(BF16)
            vc[pr] = kvc_ref[0, 0, :, vsl].astype(BF16)
            if has_prev:
                kp[pr] = kvp_ref[0, 0, :, sl].astype(BF16)
                vp[pr] = kvp_ref[0, 0, :, vsl].astype(BF16)
        qh = [jnp.where(lane_lo if h % 2 == 0 else jnp.logical_not(lane_lo),
                        q[:, (h // 2) * LANES:(h // 2 + 1) * LANES], jnp.zeros((qb, LANES), BF16)) for h in heads]
        slopes = [slope_ref[h] for h in heads]
        s_c = [_dot_nt(qh[i], kc[h // 2]) for i, h in enumerate(heads)]
        if has_prev:
            s_p = [_dot_nt(qh[i], kp[h // 2]) for i, h in enumerate(heads)]
            s = [jnp.where(live, jnp.where(lower, s_c[i], s_p[i]) - slopes[i] * dist, -jnp.inf)
                 for i in range(len(heads))]
            s_d = [jnp.where(j > 0, jnp.sum(jnp.where(diag, s_p[i], 0.0), axis=-1, keepdims=True)
                             - slopes[i] * far, -jnp.inf) for i in range(len(heads))]
            mx = [jnp.maximum(jnp.max(s[i], axis=-1, keepdims=True), s_d[i]) for i in range(len(heads))]
            p = [jnp.exp(s[i] - mx[i]) for i in range(len(heads))]
            p_d = [jnp.exp(s_d[i] - mx[i]) for i in range(len(heads))]
            den = [jnp.sum(p[i], axis=-1, keepdims=True) + p_d[i] for i in range(len(heads))]
            acc = [_dot(jnp.where(lower, p[i], 0.0).astype(BF16), vc[h // 2])
                   + _dot(jnp.where(lower, jnp.where(diag, p_d[i], 0.0), p[i]).astype(BF16), vp[h // 2])
                   for i, h in enumerate(heads)]
        else:
            s = [jnp.where(lower, s_c[i] - slopes[i] * dist, -jnp.inf) for i in range(len(heads))]
            mx = [jnp.max(s[i], axis=-1, keepdims=True) for i in range(len(heads))]
            p = [jnp.exp(s[i] - mx[i]) for i in range(len(heads))]
            den = [jnp.sum(p[i], axis=-1, keepdims=True) for i in range(len(heads))]
            acc = [_dot(p[i].astype(BF16), vc[h // 2]) for i, h in enumerate(heads)]
        out = [acc[i] / den[i] for i in range(len(heads))]
        for i, h in enumerate(heads):
            lse_tile = jnp.where(lane == h, mx[i] + jnp.log(den[i]), lse_tile)
        for i in range(0, len(heads), 2):
            pr = heads[i] // 2
            o_ref[0, 0, :, pr * LANES:(pr + 1) * LANES] = jnp.where(lane_lo, out[i], out[i + 1])
    lse_ref[0, 0] = lse_tile


def _attn_prompt_group(q, kv, slopes, g):
    win, dil = DIL_GROUPS[g]
    bsz, _, n, _ = q.shape
    assert win // dil == Q_BLOCK and n % Q_BLOCK == 0
    nblk = n // Q_BLOCK
    has_prev = nblk > 1
    prev_spec = [pl.BlockSpec((1, 1, Q_BLOCK, KV_ROW), lambda b, r, j: (b, r, jnp.maximum(j - 1, 0), 0))]
    return pl.pallas_call(
        functools.partial(_attn_prompt_kernel, dil=dil, has_prev=has_prev),
        grid=(bsz, dil, nblk),
        in_specs=[pl.BlockSpec(memory_space=pltpu.SMEM),
                  pl.BlockSpec((1, 1, Q_BLOCK, ATT_WIDTH), lambda b, r, j: (b, r, j, 0)),
                  pl.BlockSpec((1, 1, Q_BLOCK, KV_ROW), lambda b, r, j: (b, r, j, 0))]
                 + (prev_spec if has_prev else []),
        out_specs=[pl.BlockSpec((1, 1, Q_BLOCK, ATT_WIDTH), lambda b, r, j: (b, r, j, 0)),
                   pl.BlockSpec((1, 1, Q_BLOCK, LANES), lambda b, r, j: (b, r, j, 0))],
        out_shape=[jax.ShapeDtypeStruct((bsz, dil, n, ATT_WIDTH), F32),
                   jax.ShapeDtypeStruct((bsz, dil, n, LANES), F32)],
        compiler_params=_cparams(3),
    )(slopes, q, *([kv, kv] if has_prev else [kv]))


def _merge_out_kernel(o0_ref, o1_ref, o2_ref, l0_ref, l1_ref, l2_ref, gate_ref, e_ref, w_ref, r_ref,
                      g_ref, b_ref, out_ref, os_ref, ls_ref):
    lses, outs = [], []
    for g, (o_ref, l_ref) in enumerate(((o0_ref, l0_ref), (o1_ref, l1_ref), (o2_ref, l2_ref))):
        dil = DIL_GROUPS[g][1]
        if dil == 1:
            lses.append(l_ref[0, 0])
            outs.append(o_ref[0, 0])
        else:
            _interleave(ls_ref.at[g - 1], l_ref, dil)
            _interleave(os_ref.at[g - 1], o_ref, dil)
            lses.append(ls_ref[g - 1, 0])
            outs.append(_load_lane_blocked(os_ref.at[g - 1]))
    l0, l1, l2 = lses
    top = jnp.maximum(jnp.maximum(l0, l1), l2)
    w0, w1, w2 = jnp.exp(l0 - top), jnp.exp(l1 - top), jnp.exp(l2 - top)
    inv = 1.0 / (w0 + w1 + w2)
    e = e_ref[...]
    o = _exact_dot(w0 * inv, e) * outs[0] + _exact_dot(w1 * inv, e) * outs[1] + _exact_dot(w2 * inv, e) * outs[2]
    og = (o * _silu(gate_ref[...])).astype(BF16)
    v = DEEPNORM_ALPHA * r_ref[...] + _dot(og, w_ref[...])
    out_ref[...] = _layer_norm(v, g_ref[...], b_ref[...])


def _merge_out(os_, lses, gate, expand16, w, resid, g, b, seq):
    tm = PROJ_ROWS
    tpb = seq // tm
    m = resid.shape[0]
    rowblk = lambda width: pl.BlockSpec((tm, width), lambda i: (i, 0))
    const = lambda shape: pl.BlockSpec(shape, lambda i: (0, 0))
    resblk = lambda dil, width: pl.BlockSpec((1, dil, tm // dil, width), lambda i: (i // tpb, 0, i % tpb, 0))
    return pl.pallas_call(
        _merge_out_kernel,
        grid=(m // tm,),
        in_specs=[resblk(dil, ATT_WIDTH) for _, dil in DIL_GROUPS] + [resblk(dil, LANES) for _, dil in DIL_GROUPS]
                 + [rowblk(ATT_WIDTH), const((LANES, ATT_WIDTH)), const((ATT_WIDTH, D_MODEL)), rowblk(D_MODEL),
                    const((1, D_MODEL)), const((1, D_MODEL))],
        out_specs=rowblk(D_MODEL),
        out_shape=jax.ShapeDtypeStruct((m, D_MODEL), F32),
        scratch_shapes=[pltpu.VMEM((N_DIL - 1, ATT_WIDTH // LANES, tm, LANES), F32),
                        pltpu.VMEM((N_DIL - 1, 1, tm, LANES), F32)],
        compiler_params=_cparams(1),
    )(*os_, *lses, gate, expand16, w, resid, g, b)


def _cache_shift_kernel(c_ref, new_ref, sel_ref, o_ref, comp_ref, *, dil):
    b = pl.program_id(0)
    x = c_ref[...]
    win = x.shape[1]
    if dil > 1:
        comp_ref[...] = _dot(x.astype(BF16), sel_ref[...]).astype(BF16)
    else:
        comp_ref[...] = x.astype(BF16)
    nv = new_ref[...]
    pick = lax.broadcasted_iota(jnp.int32, nv.shape, 1) == b
    newcol = jnp.sum(jnp.where(pick, nv, 0.0), axis=1, keepdims=True)
    last = lax.broadcasted_iota(jnp.int32, x.shape, 1) == win - 1
    o_ref[...] = jnp.where(last, newcol, pltpu.roll(x, win - 1, axis=1))


def _cache_shift(cache, kvt_new, g):
    win, dil = DIL_GROUPS[g]
    nb = cache.shape[0]
    assert cache.shape[1] == win and win // dil == Q_BLOCK
    ct = cache.transpose(0, 2, 3, 4, 1).reshape(nb * KV_ROW, win)
    rows = min(KV_ROW, SHIFT_BLOCK_BYTES // (4 * win))
    nrb = KV_ROW // rows
    sel = (jnp.arange(win)[:, None] == jnp.arange(Q_BLOCK)[None, :] * dil).astype(BF16)
    out, comp = pl.pallas_call(
        functools.partial(_cache_shift_kernel, dil=dil),
        grid=(nb, nrb),
        in_specs=[pl.BlockSpec((rows, win), lambda b, i: (b * nrb + i, 0)),
                  pl.BlockSpec((rows, nb), lambda b, i: (g * nrb + i, 0)),
                  pl.BlockSpec((win, Q_BLOCK), lambda b, i: (0, 0))],
        out_specs=[pl.BlockSpec((rows, win), lambda b, i: (b * nrb + i, 0)),
                   pl.BlockSpec((rows, Q_BLOCK), lambda b, i: (b * nrb + i, 0))],
        out_shape=[jax.ShapeDtypeStruct((nb * KV_ROW, win), cache.dtype),
                   jax.ShapeDtypeStruct((nb * KV_ROW, Q_BLOCK), BF16)],
        compiler_params=_cparams(2),
    )(ct, kvt_new, sel)
    new_cache = out.reshape(nb, 2, ATT_HEADS, ATT_HEAD_DIM, win).transpose(0, 4, 1, 2, 3)
    return new_cache, comp


def _attn_sample_kernel(slope_ref, q0_ref, q1_ref, q2_ref, gate_ref, n0_ref, n1_ref, n2_ref,
                        c0_ref, c1_ref, c2_ref, o_ref):
    nh = ATT_HEADS
    keys = Q_BLOCK
    lane = lax.broadcasted_iota(jnp.int32, (nh, ATT_WIDTH), 1)
    hrow = lax.broadcasted_iota(jnp.int32, (nh, ATT_WIDTH), 0)
    head_mask = (lane // ATT_HEAD_DIM) == hrow
    kidx = lax.broadcasted_iota(jnp.int32, (nh, keys), 1)
    slope = slope_ref[:, 0:1]
    outs, lses = [], []
    for g, (q_ref, n_ref, c_ref) in enumerate(((q0_ref, n0_ref, c0_ref), (q1_ref, n1_ref, c1_ref),
                                               (q2_ref, n2_ref, c2_ref))):
        dil = DIL_GROUPS[g][1]
        q = q_ref[0] * ATT_SCALE
        qm = jnp.where(head_mask, jnp.broadcast_to(q, (nh, ATT_WIDTH)), 0.0)
        new = n_ref[0]
        k_t = c_ref[0:ATT_WIDTH, :]
        v_t = c_ref[ATT_WIDTH:KV_ROW, :]
        dist = ((keys - kidx) * dil).astype(F32)
        s = _dot(qm.astype(BF16), k_t) - slope * dist
        s_new = jnp.sum(qm * new[:, 0:ATT_WIDTH], axis=-1, keepdims=True)
        mx = jnp.maximum(jnp.max(s, axis=-1, keepdims=True), s_new)
        p = jnp.exp(s - mx)
        p_new = jnp.exp(s_new - mx)
        den = jnp.sum(p, axis=-1, keepdims=True) + p_new
        outs.append((_dot_nt(p.astype(BF16), v_t) + p_new * new[:, ATT_WIDTH:]) / den)
        lses.append(mx + jnp.log(den))
    top = jnp.maximum(jnp.maximum(lses[0], lses[1]), lses[2])
    ws = [jnp.exp(l - top) for l in lses]
    o = (ws[0] * outs[0] + ws[1] * outs[1] + ws[2] * outs[2]) / (ws[0] + ws[1] + ws[2])
    o = jnp.sum(jnp.where(head_mask, o, 0.0), axis=0, keepdims=True)
    o_ref[0] = o * _silu(gate_ref[0])


def _attn_sample(proj, kv_new, comps, slopes_b):
    nb = proj.shape[0]
    proj3 = proj.reshape(nb, 1, 4 * ATT_WIDTH)
    kvn3 = kv_new.reshape(nb, 1, N_DIL * KV_ROW)
    qspec = lambda g: pl.BlockSpec((1, 1, ATT_WIDTH), lambda b: (b, 0, g))
    nspec = lambda g: pl.BlockSpec((1, 1, KV_ROW), lambda b: (b, 0, g))
    cspec = pl.BlockSpec((KV_ROW, Q_BLOCK), lambda b: (b, 0))
    o = pl.pallas_call(
        _attn_sample_kernel,
        grid=(nb,),
        in_specs=[pl.BlockSpec((ATT_HEADS, LANES), lambda b: (0, 0)),
                  qspec(0), qspec(1), qspec(2), qspec(3), nspec(0), nspec(1), nspec(2),
                  cspec, cspec, cspec],
        out_specs=pl.BlockSpec((1, 1, ATT_WIDTH), lambda b: (b, 0, 0)),
        out_shape=jax.ShapeDtypeStruct((nb, 1, ATT_WIDTH), F32),
        compiler_params=_cparams(1),
    )(slopes_b, proj3, proj3, proj3, proj3, kvn3, kvn3, kvn3, *comps)
    return o.reshape(nb, ATT_WIDTH)


def _pad_lanes(v):
    return jnp.pad(v.astype(F32), (0, LANES - v.shape[0])).reshape(1, LANES)


def kernel(x_prompt, x_sample, state_ssm, state_conv, cache_kv_w128, cache_kv_w512, cache_kv_w2048,
           a_in_proj, a_conv_w, a_conv_b, a_dt_bias, a_log, a_d, a_norm_w, a_out_proj,
           kv_proj, b_in_proj, b_out_proj, ln_g, ln_b):
    bsz, seq, _ = x_prompt.shape
    nb = x_sample.shape[0]
    assert x_sample.shape[1] == 1 and seq % CHUNK == 0
    caches = (cache_kv_w128, cache_kv_w512, cache_kv_w2048)

    heads = jnp.arange(LANES)[:, None]
    expand32 = (heads == jnp.arange(D_INNER)[None, :] // SSM_HEAD_DIM).astype(BF16)
    expand16 = (heads == jnp.arange(ATT_WIDTH)[None, :] // ATT_HEAD_DIM).astype(BF16)
    tril = (jnp.arange(CHUNK)[:, None] >= jnp.arange(CHUNK)[None, :]).astype(BF16)
    eye = jnp.eye(LANES, dtype=BF16)
    slopes = jnp.exp2(-8.0 * jnp.arange(1, ATT_HEADS + 1, dtype=F32) / ATT_HEADS)
    slopes_b = jnp.broadcast_to(slopes[:, None], (ATT_HEADS, LANES))

    hp = x_prompt.reshape(bsz * seq, D_MODEL)
    hs = x_sample.reshape(nb, D_MODEL)
    ssm_p, conv_p, ssm_s, conv_s = [], [], [], []

    for i in range(N_A_LAYERS):
        w_in = a_in_proj[i]
        w_main = w_in[:, 0:D_INNER + CONV_DIM].astype(BF16)
        w_dt = jnp.pad(w_in[:, D_INNER + CONV_DIM:], ((0, 0), (0, LANES - SSM_HEADS))).astype(BF16)
        w_out = a_out_proj[i].astype(BF16)
        prm = dict(
            cwx=a_conv_w[i][:, 0:D_INNER], cwbc=a_conv_w[i][:, D_INNER:],
            cbx=a_conv_b[i][0:D_INNER].reshape(1, D_INNER), cbbc=a_conv_b[i][D_INNER:].reshape(1, BC_WIDTH),
            dtb=_pad_lanes(a_dt_bias[i]), alog=_pad_lanes(a_log[i]),
            dexp=jnp.repeat(a_d[i].astype(F32), SSM_HEAD_DIM).reshape(1, D_INNER),
            nw=a_norm_w[i].reshape(1, D_INNER), expand=expand32, tril=tril, eye=eye)
        g_ln, b_ln = ln_g[i].reshape(1, D_MODEL), ln_b[i].reshape(1, D_MODEL)

        zx = _matmul(hp, w_main, tn=512)
        dtp = _matmul(hp, w_dt, tn=LANES)
        yn, ssm_new, conv_new = _ssd_prompt(zx, dtp, prm, bsz, seq)
        hp = _matmul_ln(yn, w_out, hp, g_ln, b_ln)
        ssm_p.append(ssm_new)
        conv_p.append(conv_new)

        zx_s = _matmul(hs, w_main, tn=512)
        dt_s = _matmul(hs, w_dt, tn=LANES)
        yn_s, conv_new_s, ssm_new_s = _ssd_sample(zx_s, dt_s, state_conv[i], state_ssm[i], prm)
        hs = _matmul_ln(yn_s, w_out, hs, g_ln, b_ln)
        ssm_s.append(ssm_new_s)
        conv_s.append(conv_new_s)

    kvw = kv_proj.reshape(D_MODEL, 2, N_DIL, ATT_WIDTH)
    kvw_g = [jnp.concatenate([kvw[:, 0, g], kvw[:, 1, g]], axis=1).astype(BF16) for g in range(N_DIL)]
    kvw_all = jnp.concatenate(kvw_g, axis=1)
    kv_p, new_kv_p = [], []
    for g in range(N_DIL):
        kv_res, kv_t = _kv_proj(hp, kvw_g[g], g, bsz, seq)
        kv_p.append(kv_res)
        new_kv_p.append(kv_t.reshape(bsz, 2, ATT_HEADS, ATT_HEAD_DIM, kv_t.shape[-1]).transpose(0, 4, 1, 2, 3))
    kv_s = _matmul(hs, kvw_all, tn=1024)
    kvt_s = _matmul_nt(kvw_all.T, hs, tn=1024)
    new_kv_s, comps = [], []
    for g in range(N_DIL):
        shifted, comp = _cache_shift(caches[g], kvt_s, g)
        new_kv_s.append(shifted)
        comps.append(comp)

    for j in range(N_B_LAYERS):
        layer = N_A_LAYERS + j
        w_in = b_in_proj[j].astype(BF16)
        w_out = b_out_proj[j].astype(BF16)
        g_ln, b_ln = ln_g[layer].reshape(1, D_MODEL), ln_b[layer].reshape(1, D_MODEL)

        q0, q1, q2, gate = _b_in_proj(hp, w_in, bsz, seq)
        res = [_attn_prompt_group(q, kv_p[g], slopes, g) for g, q in enumerate((q0, q1, q2))]
        hp = _merge_out([r[0] for r in res], [r[1] for r in res], gate, expand16, w_out, hp, g_ln, b_ln, seq)

        proj_s = _matmul(hs, w_in, tn=1024)
        og_s = _attn_sample(proj_s, kv_s, comps, slopes_b)
        hs = _matmul_ln(og_s, w_out, hs, g_ln, b_ln)

    return (hp.reshape(bsz, seq, D_MODEL), hs.reshape(nb, 1, D_MODEL),
            jnp.stack(ssm_p), jnp.stack(conv_p), new_kv_p[0], new_kv_p[1], new_kv_p[2],
            jnp.stack(ssm_s), jnp.stack(conv_s), new_kv_s[0], new_kv_s[1], new_kv_s[2])
```

```python
import functools

import jax
import jax.numpy as jnp
from jax import lax
from jax.experimental import pallas as pl
from jax.experimental.pallas import tpu as pltpu

F32 = jnp.float32
BF16 = jnp.bfloat16

D_MODEL = 1024
N_A_LAYERS = 2
N_B_LAYERS = 2
D_INNER = 2048
SSM_HEAD_DIM = 64
SSM_HEADS = 32
SSM_GROUPS = 4
SSM_STATE = 128
HEADS_PER_GROUP = SSM_HEADS // SSM_GROUPS
GROUP_WIDTH = D_INNER // SSM_GROUPS
CONV_WIDTH = 4
BC_WIDTH = 2 * SSM_GROUPS * SSM_STATE
CONV_DIM = D_INNER + BC_WIDTH
CHUNK = 128
DIL_GROUPS = ((128, 1), (512, 4), (2048, 16))
N_DIL = 3
ATT_HEADS = 16
ATT_HEAD_DIM = 64
ATT_WIDTH = 1024
KV_ROW = 2 * ATT_WIDTH
Q_BLOCK = 128
ATT_SCALE = ATT_HEAD_DIM ** -0.5
LN_EPS = 1e-5
RMS_EPS = 1e-5
DEEPNORM_ALPHA = (2.0 * 4) ** 0.25

LANES = 128
SUBLANES = 8
VMEM_LIMIT = 48 * 1024 * 1024

ATT_HEAD_BATCH = 16
PROJ_ROWS = 512
PROJ_COLS = 512
KV_ROWS = 256
SHIFT_BLOCK_BYTES = 4 * 1024 * 1024


def _cparams(n_grid):
    return pltpu.CompilerParams(dimension_semantics=("arbitrary",) * n_grid, vmem_limit_bytes=VMEM_LIMIT)


def _silu(x):
    h = 0.5 * x
    return h + h * jnp.tanh(h)


def _softplus(x):
    return jnp.maximum(x, 0.0) + jnp.log1p(jnp.exp(-jnp.abs(x)))


def _split3(a):
    hi = a.astype(BF16)
    r1 = a - hi.astype(F32)
    mid = r1.astype(BF16)
    lo = (r1 - mid.astype(F32)).astype(BF16)
    return hi, mid, lo


def _dot(a, b):
    return jnp.dot(a, b, preferred_element_type=F32)


def _dot_nt(a, b):
    return lax.dot_general(a, b, (((1,), (1,)), ((), ())), preferred_element_type=F32)


def _expand_dot(a, m01_twice):
    hi = a.astype(BF16)
    lo = (a - hi.astype(F32)).astype(BF16)
    return _dot(jnp.concatenate([hi, lo], axis=1), m01_twice)


def _exact_dot_left(m01, a):
    hi, mid, lo = _split3(a)
    return _dot(m01, hi) + _dot(m01, mid) + _dot(m01, lo)


def _layer_norm(v, g, b):
    mu = jnp.mean(v, axis=-1, keepdims=True)
    d = v - mu
    var = jnp.mean(d * d, axis=-1, keepdims=True)
    return d * lax.rsqrt(var + LN_EPS) * g + b


def _lane_blocks(width):
    return [slice(cb * LANES, (cb + 1) * LANES) for cb in range(width // LANES)]


def _store_lane_blocked(blk_ref, val):
    for cb, sl in enumerate(_lane_blocks(val.shape[1])):
        blk_ref[cb] = val[:, sl]


def _load_lane_blocked(blk_ref):
    return jnp.concatenate([blk_ref[cb] for cb in range(blk_ref.shape[0])], axis=1)


def _deinterleave(dst_ref, blk_ref, dil):
    rows = blk_ref.shape[1] // dil
    for r in range(dil):
        for cb, sl in enumerate(_lane_blocks(dst_ref.shape[-1])):
            dst_ref[0, r, :, sl] = blk_ref[cb, pl.ds(r, rows, stride=dil), :].astype(dst_ref.dtype)


def _interleave(blk_ref, src_ref, dil):
    rows = blk_ref.shape[1] // dil
    for r in range(dil):
        for cb, sl in enumerate(_lane_blocks(src_ref.shape[-1])):
            blk_ref[cb, pl.ds(r, rows, stride=dil), :] = src_ref[0, r, :, sl].astype(blk_ref.dtype)


def _mm_kernel(x_ref, w_ref, o_ref, xb_ref):
    @pl.when(pl.program_id(1) == 0)
    def _():
        xb_ref[...] = x_ref[...].astype(BF16)

    o_ref[...] = _dot(xb_ref[...], w_ref[...]).astype(o_ref.dtype)


def _matmul(x, w, *, tn, out_dtype=F32, tm=1024):
    m, k = x.shape
    n = w.shape[1]
    tm = min(tm, m)
    assert m % tm == 0 and n % tn == 0
    return pl.pallas_call(
        _mm_kernel,
        grid=(m // tm, n // tn),
        in_specs=[pl.BlockSpec((tm, k), lambda i, j: (i, 0)),
                  pl.BlockSpec((k, tn), lambda i, j: (0, j))],
        out_specs=pl.BlockSpec((tm, tn), lambda i, j: (i, j)),
        out_shape=jax.ShapeDtypeStruct((m, n), out_dtype),
        scratch_shapes=[pltpu.VMEM((tm, k), BF16)],
        compiler_params=_cparams(2),
    )(x, w)


def _a_in_proj_kernel(x_ref, w_ref, wdt_ref, zx_ref, dt_ref):
    xb = x_ref[...].astype(BF16)
    for c in range(w_ref.shape[1] // PROJ_COLS):
        sl = slice(c * PROJ_COLS, (c + 1) * PROJ_COLS)
        zx_ref[:, sl] = _dot(xb, w_ref[:, sl]).astype(zx_ref.dtype)
    dt_ref[...] = _dot(xb, wdt_ref[...])


def _a_in_proj(x, w_main, w_dt, out_dtype):
    m, k = x.shape
    n = w_main.shape[1]
    tm = min(PROJ_ROWS, m)
    assert m % tm == 0 and n % PROJ_COLS == 0
    return pl.pallas_call(
        _a_in_proj_kernel,
        grid=(m // tm,),
        in_specs=[pl.BlockSpec((tm, k), lambda i: (i, 0)),
                  pl.BlockSpec((k, n), lambda i: (0, 0)),
                  pl.BlockSpec((k, LANES), lambda i: (0, 0))],
        out_specs=[pl.BlockSpec((tm, n), lambda i: (i, 0)),
                   pl.BlockSpec((tm, LANES), lambda i: (i, 0))],
        out_shape=[jax.ShapeDtypeStruct((m, n), out_dtype),
                   jax.ShapeDtypeStruct((m, LANES), F32)],
        compiler_params=_cparams(1),
    )(x, w_main, w_dt)


def _mm_nt_kernel(w_ref, x_ref, o_ref):
    o_ref[...] = _dot_nt(w_ref[...], x_ref[...].astype(BF16))


def _matmul_nt(w_t, x, *, tn):
    n, k = w_t.shape
    m = x.shape[0]
    assert n % tn == 0
    return pl.pallas_call(
        _mm_nt_kernel,
        grid=(n // tn,),
        in_specs=[pl.BlockSpec((tn, k), lambda i: (i, 0)),
                  pl.BlockSpec((m, k), lambda i: (0, 0))],
        out_specs=pl.BlockSpec((tn, m), lambda i: (i, 0)),
        out_shape=jax.ShapeDtypeStruct((n, m), F32),
        compiler_params=_cparams(1),
    )(w_t, x)


def _mm_ln_kernel(y_ref, w_ref, r_ref, g_ref, b_ref, o_ref):
    acc = _dot(y_ref[...].astype(BF16), w_ref[...])
    v = DEEPNORM_ALPHA * r_ref[...] + acc
    o_ref[...] = _layer_norm(v, g_ref[...], b_ref[...])


def _matmul_ln(y, w, resid, g, b, *, tm=512):
    m, k = y.shape
    n = w.shape[1]
    tm = min(tm, m)
    assert m % tm == 0
    return pl.pallas_call(
        _mm_ln_kernel,
        grid=(m // tm,),
        in_specs=[pl.BlockSpec((tm, k), lambda i: (i, 0)),
                  pl.BlockSpec((k, n), lambda i: (0, 0)),
                  pl.BlockSpec((tm, n), lambda i: (i, 0)),
                  pl.BlockSpec((1, n), lambda i: (0, 0)),
                  pl.BlockSpec((1, n), lambda i: (0, 0))],
        out_specs=pl.BlockSpec((tm, n), lambda i: (i, 0)),
        out_shape=jax.ShapeDtypeStruct((m, n), F32),
        compiler_params=_cparams(1),
    )(y, w, resid, g, b)


def _ssd_prompt_kernel(z_ref, xr_ref, bcr_ref, dtr_ref, cwx_ref, cwbc_ref, cbx_ref, cbbc_ref,
                       dtb_ref, alog_ref, dexp_ref, nw_ref, e_ref, tril_ref, shift_ref,
                       y_ref, ssm_ref, conv_ref,
                       st_ref, extx_ref, extbc_ref):
    c = pl.program_id(1)
    t = CHUNK

    @pl.when(c == 0)
    def _():
        st_ref[...] = jnp.zeros_like(st_ref)
        extx_ref[0:t, :] = jnp.zeros((t, D_INNER), BF16)
        extbc_ref[0:t, :] = jnp.zeros((t, BC_WIDTH), BF16)

    extx_ref[t:2 * t, :] = xr_ref[...]
    extbc_ref[t:2 * t, :] = bcr_ref[...]

    def conv(ext_ref, w_ref, b_ref):
        taps = _dot(shift_ref[...], ext_ref[...])
        acc = b_ref[...]
        for k in range(CONV_WIDTH):
            acc = acc + taps[k * t:(k + 1) * t] * w_ref[k:k + 1, :]
        return acc, taps[CONV_WIDTH * t - (CONV_WIDTH - 1):CONV_WIDTH * t]

    xc, x_tail = conv(extx_ref, cwx_ref, cbx_ref)
    bcc, bc_tail = conv(extbc_ref, cwbc_ref, cbbc_ref)
    xs = _silu(xc)
    bc = _silu(bcc)
    conv_ref[0, 0, :, 0:D_INNER] = x_tail
    conv_ref[0, 0, :, D_INNER:CONV_DIM] = bc_tail
    extx_ref[0:t, :] = xr_ref[...]
    extbc_ref[0:t, :] = bcr_ref[...]

    bm = bc[:, 0:SSM_GROUPS * SSM_STATE]
    cm = bc[:, SSM_GROUPS * SSM_STATE:]

    dt = _softplus(dtr_ref[...] + dtb_ref[...])
    a = -jnp.exp(alog_ref[...])
    acs = _exact_dot_left(tril_ref[...], dt * a)
    acs_t = acs.T
    a_last = acs[t - 1:t, :]
    stacked = jnp.concatenate(
        [dt, jnp.exp(acs), jnp.exp(a_last - acs), jnp.broadcast_to(jnp.exp(a_last), (SUBLANES, LANES))], axis=0)
    ex = _expand_dot(stacked, e_ref[...])
    dt_e = ex[0:t]
    ea_e = ex[t:2 * t]
    te_e = ex[2 * t:3 * t]
    cd_e = ex[3 * t:3 * t + 1]

    xdt = xs * dt_e
    xdt_b = xdt.astype(BF16)
    xw_b = (xdt * te_e).astype(BF16)

    row = lax.broadcasted_iota(jnp.int32, (t, t), 0)
    col = lax.broadcasted_iota(jnp.int32, (t, t), 1)
    causal = row >= col
    lane_lo = lax.broadcasted_iota(jnp.int32, (t, LANES), 1) < SSM_HEAD_DIM

    y_groups = []
    for g in range(SSM_GROUPS):
        g0 = g * GROUP_WIDTH
        cg_b = cm[:, g * SSM_STATE:(g + 1) * SSM_STATE].astype(BF16)
        bg = bm[:, g * SSM_STATE:(g + 1) * SSM_STATE]
        cb = _dot_nt(cg_b, bg.astype(BF16))
        s_old = st_ref[:, g0:g0 + GROUP_WIDTH]
        y_off = _dot(cg_b, s_old.astype(BF16)) * ea_e[:, g0:g0 + GROUP_WIDTH]
        states = _dot(bg.T.astype(BF16), xw_b[:, g0:g0 + GROUP_WIDTH])
        st_ref[:, g0:g0 + GROUP_WIDTH] = s_old * cd_e[:, g0:g0 + GROUP_WIDTH] + states
        pairs = []
        for pr in range(HEADS_PER_GROUP // 2):
            xp = xdt_b[:, g0 + pr * LANES:g0 + (pr + 1) * LANES]
            halves = []
            for half in range(2):
                h = g * HEADS_PER_GROUP + 2 * pr + half
                seg = acs[:, h:h + 1] - acs_t[h:h + 1, :]
                dec = jnp.exp(jnp.where(causal, seg, -jnp.inf))
                halves.append(_dot((cb * dec).astype(BF16), xp))
            pairs.append(jnp.where(lane_lo, halves[0], halves[1]))
        y_groups.append(jnp.concatenate(pairs, axis=1) + y_off)
    y = jnp.concatenate(y_groups, axis=1) + xs * dexp_ref[...]

    hz = y * _silu(z_ref[...].astype(F32))
    normed = []
    for g in range(SSM_GROUPS):
        hg = hz[:, g * GROUP_WIDTH:(g + 1) * GROUP_WIDTH]
        normed.append(hg * lax.rsqrt(jnp.mean(hg * hg, axis=-1, keepdims=True) + RMS_EPS))
    y_ref[...] = (jnp.concatenate(normed, axis=1) * nw_ref[...]).astype(y_ref.dtype)

    @pl.when(c == pl.num_programs(1) - 1)
    def _():
        for j in range(D_INNER // LANES):
            tile = st_ref[:, j * LANES:(j + 1) * LANES].T
            ssm_ref[0, 0, 2 * j:2 * j + 2] = tile.reshape(2, SSM_HEAD_DIM, SSM_STATE)


def _skip_carried(body, n_in, n_carry):
    def kern(*refs):
        body(*refs[:n_in], *refs[n_in + n_carry:])
    return kern


def _ssd_prompt(zx, dtp, prm, bsz, seq, layer, carried):
    nc = seq // CHUNK
    m = bsz * seq
    row = lambda b, c: b * nc + c
    const = lambda shape: pl.BlockSpec(shape, lambda b, c: (0,) * len(shape))
    ins = [zx, zx, zx, dtp, prm["cwx"], prm["cwbc"], prm["cbx"], prm["cbbc"], prm["dtb"], prm["alog"],
           prm["dexp"], prm["nw"], prm["expand2"], prm["tril"], prm["shift"]]
    extra = [] if carried is None else list(carried)
    aliases = {} if carried is None else {len(ins): 1, len(ins) + 1: 2}
    return pl.pallas_call(
        _skip_carried(_ssd_prompt_kernel, len(ins), len(extra)),
        grid=(bsz, nc),
        in_specs=[pl.BlockSpec((CHUNK, D_INNER), lambda b, c: (row(b, c), 0)),
                  pl.BlockSpec((CHUNK, D_INNER), lambda b, c: (row(b, c), 1)),
                  pl.BlockSpec((CHUNK, BC_WIDTH), lambda b, c: (row(b, c), 4)),
                  pl.BlockSpec((CHUNK, LANES), lambda b, c: (row(b, c), 0)),
                  const((CONV_WIDTH, D_INNER)), const((CONV_WIDTH, BC_WIDTH)),
                  const((1, D_INNER)), const((1, BC_WIDTH)),
                  const((1, LANES)), const((1, LANES)),
                  const((1, D_INNER)), const((1, D_INNER)),
                  const((2 * LANES, D_INNER)), const((CHUNK, CHUNK)), const((CONV_WIDTH * CHUNK, 2 * CHUNK))]
                 + [pl.BlockSpec(memory_space=pl.ANY)] * len(extra),
        out_specs=[pl.BlockSpec((CHUNK, D_INNER), lambda b, c: (row(b, c), 0)),
                   pl.BlockSpec((1, 1, SSM_HEADS, SSM_HEAD_DIM, SSM_STATE), lambda b, c: (layer, b, 0, 0, 0)),
                   pl.BlockSpec((1, 1, CONV_WIDTH - 1, CONV_DIM), lambda b, c: (layer, b, 0, 0))],
        out_shape=[jax.ShapeDtypeStruct((m, D_INNER), BF16),
                   jax.ShapeDtypeStruct((N_A_LAYERS, bsz, SSM_HEADS, SSM_HEAD_DIM, SSM_STATE), F32),
                   jax.ShapeDtypeStruct((N_A_LAYERS, bsz, CONV_WIDTH - 1, CONV_DIM), F32)],
        scratch_shapes=[pltpu.VMEM((SSM_STATE, D_INNER), F32),
                        pltpu.VMEM((2 * CHUNK, D_INNER), BF16),
                        pltpu.VMEM((2 * CHUNK, BC_WIDTH), BF16)],
        input_output_aliases=aliases,
        compiler_params=_cparams(2),
    )(*ins, *extra)


def _ssd_sample_kernel(z_ref, xr_ref, bcr_ref, dtr_ref, conv_ref, ssm_ref,
                       cwx_ref, cwbc_ref, cbx_ref, cbbc_ref, dtb_ref, alog_ref, dexp_ref, nw_ref,
                       e_ref, eye_ref,
                       y_ref, convo_ref, ssmo_ref, st_ref):
    prev = conv_ref[0, 0]
    raw = jnp.concatenate([xr_ref[0], bcr_ref[0]], axis=1)
    cw = jnp.concatenate([cwx_ref[...], cwbc_ref[...]], axis=1)
    cbias = jnp.concatenate([cbx_ref[...], cbbc_ref[...]], axis=1)
    acc = cbias + raw * cw[CONV_WIDTH - 1:CONV_WIDTH, :]
    for k in range(CONV_WIDTH - 1):
        acc = acc + prev[k:k + 1, :] * cw[k:k + 1, :]
    convo_ref[0, 0, 0:CONV_WIDTH - 2, :] = prev[1:CONV_WIDTH - 1, :]
    convo_ref[0, 0, CONV_WIDTH - 2:CONV_WIDTH - 1, :] = raw
    xbc = _silu(acc)
    xs = xbc[:, 0:D_INNER]
    bm = xbc[:, D_INNER:D_INNER + SSM_GROUPS * SSM_STATE]
    cm = xbc[:, D_INNER + SSM_GROUPS * SSM_STATE:]

    dt = _softplus(dtr_ref[0] + dtb_ref[...])
    dec = jnp.exp(dt * -jnp.exp(alog_ref[...]))
    stacked = jnp.concatenate([dt, dec, jnp.zeros((SUBLANES - 2, LANES), F32)], axis=0)
    ex = _expand_dot(stacked, e_ref[...])
    dt_e = ex[0:1]
    dec_e = ex[1:2]
    xdt = xs * dt_e

    rows = jnp.concatenate([bm[:, g * SSM_STATE:(g + 1) * SSM_STATE] for g in range(SSM_GROUPS)]
                           + [cm[:, g * SSM_STATE:(g + 1) * SSM_STATE] for g in range(SSM_GROUPS)], axis=0)
    hi, mid, lo = _split3(rows)
    eye = eye_ref[...]
    cols = _dot_nt(eye, hi) + _dot_nt(eye, mid) + _dot_nt(eye, lo)

    for j in range(D_INNER // LANES):
        tile = ssm_ref[0, 0, 2 * j:2 * j + 2].reshape(LANES, SSM_STATE)
        st_ref[:, j * LANES:(j + 1) * LANES] = tile.T
    y_groups = []
    for g in range(SSM_GROUPS):
        g0 = g * GROUP_WIDTH
        new = (st_ref[:, g0:g0 + GROUP_WIDTH] * dec_e[:, g0:g0 + GROUP_WIDTH]
               + cols[:, g:g + 1] * xdt[:, g0:g0 + GROUP_WIDTH])
        st_ref[:, g0:g0 + GROUP_WIDTH] = new
        y_groups.append(jnp.sum(new * cols[:, SSM_GROUPS + g:SSM_GROUPS + g + 1], axis=0, keepdims=True))
    for j in range(D_INNER // LANES):
        tile = st_ref[:, j * LANES:(j + 1) * LANES].T
        ssmo_ref[0, 0, 2 * j:2 * j + 2] = tile.reshape(2, SSM_HEAD_DIM, SSM_STATE)
    y = jnp.concatenate(y_groups, axis=1) + xs * dexp_ref[...]

    hz = y * _silu(z_ref[0])
    normed = []
    for g in range(SSM_GROUPS):
        hg = hz[:, g * GROUP_WIDTH:(g + 1) * GROUP_WIDTH]
        normed.append(hg * lax.rsqrt(jnp.mean(hg * hg, axis=-1, keepdims=True) + RMS_EPS))
    y_ref[0] = jnp.concatenate(normed, axis=1) * nw_ref[...]


def _ssd_sample(zx, dts, conv_state, ssm_state, prm, layer, carried):
    nb = zx.shape[0]
    zx3 = zx.reshape(nb, 1, zx.shape[1])
    dt3 = dts.reshape(nb, 1, LANES)
    const = lambda shape: pl.BlockSpec(shape, lambda b: (0,) * len(shape))
    conv_blk = pl.BlockSpec((1, 1, CONV_WIDTH - 1, CONV_DIM), lambda b: (layer, b, 0, 0))
    ssm_blk = pl.BlockSpec((1, 1, SSM_HEADS, SSM_HEAD_DIM, SSM_STATE), lambda b: (layer, b, 0, 0, 0))
    ins = [zx3, zx3, zx3, dt3, conv_state, ssm_state, prm["cwx"], prm["cwbc"], prm["cbx"], prm["cbbc"],
           prm["dtb"], prm["alog"], prm["dexp"], prm["nw"], prm["expand2"], prm["eye"]]
    extra = [] if carried is None else list(carried)
    aliases = {} if carried is None else {len(ins): 1, len(ins) + 1: 2}
    y, conv_new, ssm_new = pl.pallas_call(
        _skip_carried(_ssd_sample_kernel, len(ins), len(extra)),
        grid=(nb,),
        in_specs=[pl.BlockSpec((1, 1, D_INNER), lambda b: (b, 0, 0)),
                  pl.BlockSpec((1, 1, D_INNER), lambda b: (b, 0, 1)),
                  pl.BlockSpec((1, 1, BC_WIDTH), lambda b: (b, 0, 4)),
                  pl.BlockSpec((1, 1, LANES), lambda b: (b, 0, 0)),
                  conv_blk, ssm_blk,
                  const((CONV_WIDTH, D_INNER)), const((CONV_WIDTH, BC_WIDTH)),
                  const((1, D_INNER)), const((1, BC_WIDTH)),
                  const((1, LANES)), const((1, LANES)),
                  const((1, D_INNER)), const((1, D_INNER)),
                  const((2 * LANES, D_INNER)), const((LANES, LANES))]
                 + [pl.BlockSpec(memory_space=pl.ANY)] * len(extra),
        out_specs=[pl.BlockSpec((1, 1, D_INNER), lambda b: (b, 0, 0)), conv_blk, ssm_blk],
        out_shape=[jax.ShapeDtypeStruct((nb, 1, D_INNER), F32),
                   jax.ShapeDtypeStruct(conv_state.shape, F32),
                   jax.ShapeDtypeStruct(ssm_state.shape, F32)],
        scratch_shapes=[pltpu.VMEM((SSM_STATE, D_INNER), F32)],
        input_output_aliases=aliases,
        compiler_params=_cparams(1),
    )(*ins, *extra)
    return y.reshape(nb, D_INNER), conv_new, ssm_new


def _b_in_proj_kernel(x_ref, w_ref, q0_ref, q1_ref, q2_ref, gate_ref, acc_ref):
    xb = x_ref[...].astype(BF16)
    for g, dst in enumerate((q0_ref, q1_ref, q2_ref)):
        acc = _dot(xb, w_ref[:, g * ATT_WIDTH:(g + 1) * ATT_WIDTH])
        if DIL_GROUPS[g][1] == 1:
            dst[0, 0] = acc.astype(dst.dtype)
        else:
            _store_lane_blocked(acc_ref, acc)
            _deinterleave(dst, acc_ref, DIL_GROUPS[g][1])
    gate_ref[...] = _dot(xb, w_ref[:, N_DIL * ATT_WIDTH:])


def _b_in_proj(x, w, bsz, seq):
    tm = PROJ_ROWS
    tpb = seq // tm
    m, k = x.shape
    qspecs, qshapes = [], []
    for _, dil in DIL_GROUPS:
        assert tm % (dil * 2 * SUBLANES) == 0
        qspecs.append(pl.BlockSpec((1, dil, tm // dil, ATT_WIDTH), lambda i: (i // tpb, 0, i % tpb, 0)))
        qshapes.append(jax.ShapeDtypeStruct((bsz, dil, seq // dil, ATT_WIDTH), BF16))
    return pl.pallas_call(
        _b_in_proj_kernel,
        grid=(m // tm,),
        in_specs=[pl.BlockSpec((tm, k), lambda i: (i, 0)),
                  pl.BlockSpec(w.shape, lambda i: (0, 0))],
        out_specs=qspecs + [pl.BlockSpec((tm, ATT_WIDTH), lambda i: (i, 0))],
        out_shape=qshapes + [jax.ShapeDtypeStruct((m, ATT_WIDTH), F32)],
        scratch_shapes=[pltpu.VMEM((ATT_WIDTH // LANES, tm, LANES), F32)],
        compiler_params=_cparams(1),
    )(x, w)


def _kv_proj_kernel(x_ref, w_ref, kv_ref, kvt_ref, acc_ref, *, dil, first_tile, wt):
    t = pl.program_id(1)
    acc = _dot(x_ref[...].astype(BF16), w_ref[...])
    _store_lane_blocked(acc_ref, acc)
    if dil == 1:
        kv_ref[0, 0] = acc.astype(kv_ref.dtype)
    else:
        _deinterleave(kv_ref, acc_ref, dil)

    @pl.when(t >= first_tile)
    def _():
        tm = acc_ref.shape[1]
        for cb in range(KV_ROW // LANES):
            for rb in range(wt // LANES):
                r0 = tm - wt + rb * LANES
                kvt_ref[0, cb * LANES:(cb + 1) * LANES, rb * LANES:(rb + 1) * LANES] = (
                    acc_ref[cb, r0:r0 + LANES, :].T)


def _kv_proj(x, w, g, bsz, seq):
    win, dil = DIL_GROUPS[g]
    win = min(win, seq)
    tm = KV_ROWS
    tpb = seq // tm
    wt = min(win, tm)
    first_tile = tpb - win // wt
    assert tm % (dil * 2 * SUBLANES) == 0 and win % wt == 0 and seq % tm == 0
    k = x.shape[1]
    return pl.pallas_call(
        functools.partial(_kv_proj_kernel, dil=dil, first_tile=first_tile, wt=wt),
        grid=(bsz, tpb),
        in_specs=[pl.BlockSpec((tm, k), lambda b, t: (b * tpb + t, 0)),
                  pl.BlockSpec((k, KV_ROW), lambda b, t: (0, 0))],
        out_specs=[pl.BlockSpec((1, dil, tm // dil, KV_ROW), lambda b, t: (b, 0, t, 0)),
                   pl.BlockSpec((1, KV_ROW, wt), lambda b, t: (b, 0, jnp.maximum(t - first_tile, 0)))],
        out_shape=[jax.ShapeDtypeStruct((bsz, dil, seq // dil, KV_ROW), BF16),
                   jax.ShapeDtypeStruct((bsz, KV_ROW, win), F32)],
        scratch_shapes=[pltpu.VMEM((KV_ROW // LANES, tm, LANES), F32)],
        compiler_params=_cparams(2),
    )(x, w)


def _attn_prompt_kernel(slope_ref, q_ref, kvc_ref, o_ref, lse_ref, *scratch, dil, has_prev):
    j = pl.program_id(2)
    qb = Q_BLOCK
    if has_prev:
        kvp_ref, = scratch

        @pl.when(j == 0)
        def _():
            kvp_ref[...] = jnp.zeros_like(kvp_ref)

    q = (q_ref[0, 0].astype(F32) * ATT_SCALE).astype(BF16)
    row = lax.broadcasted_iota(jnp.int32, (qb, qb), 0)
    col = lax.broadcasted_iota(jnp.int32, (qb, qb), 1)
    lower = col <= row
    diag = col == row
    live = jnp.logical_or(lower, j > 0) if has_prev else lower
    dist = jnp.where(live, (((row - col) & (qb - 1)) * dil).astype(F32), jnp.inf)
    far = float(qb * dil)
    lane = lax.broadcasted_iota(jnp.int32, (qb, LANES), 1)
    lane_lo = lane < ATT_HEAD_DIM
    lse_tile = jnp.zeros((qb, LANES), F32)

    for hb in range(0, ATT_HEADS, ATT_HEAD_BATCH):
        heads = list(range(hb, hb + ATT_HEAD_BATCH))
        kc, vc, kp, vp = {}, {}, {}, {}
        for pr in sorted({h // 2 for h in heads}):
            sl = slice(pr * LANES, (pr + 1) * LANES)
            vsl = slice(ATT_WIDTH + pr * LANES, ATT_WIDTH + (pr + 1) * LANES)
            kc[pr] = kvc_ref[0, 0, :, sl]
            vc[pr] = kvc_ref[0, 0, :, vsl]
            if has_prev:
                kp[pr] = kvp_ref[:, sl]
                vp[pr] = kvp_ref[:, vsl]
        qh = [jnp.where(lane_lo if h % 2 == 0 else jnp.logical_not(lane_lo),
                        q[:, (h // 2) * LANES:(h // 2 + 1) * LANES], jnp.zeros((qb, LANES), BF16)) for h in heads]
        slopes = [slope_ref[h] for h in heads]
        s_c = [_dot_nt(qh[i], kc[h // 2]) for i, h in enumerate(heads)]
        if has_prev:
            s_p = [_dot_nt(qh[i], kp[h // 2]) for i, h in enumerate(heads)]
            s = [jnp.where(lower, s_c[i], s_p[i]) - slopes[i] * dist for i in range(len(heads))]
            s_d = [jnp.where(j > 0, jnp.sum(jnp.where(diag, s_p[i], 0.0), axis=-1, keepdims=True)
                             - slopes[i] * far, -jnp.inf) for i in range(len(heads))]
            mx = [jnp.maximum(jnp.max(s[i], axis=-1, keepdims=True), s_d[i]) for i in range(len(heads))]
            p = [jnp.exp(s[i] - mx[i]) for i in range(len(heads))]
            p_d = [jnp.exp(s_d[i] - mx[i]) for i in range(len(heads))]
            den = [jnp.sum(p[i], axis=-1, keepdims=True) + p_d[i] for i in range(len(heads))]
            acc = [_dot(jnp.where(lower, p[i], 0.0).astype(BF16), vc[h // 2])
                   + _dot(jnp.where(lower, jnp.where(diag, p_d[i], 0.0), p[i]).astype(BF16), vp[h // 2])
                   for i, h in enumerate(heads)]
        else:
            s = [s_c[i] - slopes[i] * dist for i in range(len(heads))]
            mx = [jnp.max(s[i], axis=-1, keepdims=True) for i in range(len(heads))]
            p = [jnp.exp(s[i] - mx[i]) for i in range(len(heads))]
            den = [jnp.sum(p[i], axis=-1, keepdims=True) for i in range(len(heads))]
            acc = [_dot(p[i].astype(BF16), vc[h // 2]) for i, h in enumerate(heads)]
        out = [acc[i] * (1.0 / den[i]) for i in range(len(heads))]
        for i, h in enumerate(heads):
            lse_tile = jnp.where(lane == h, mx[i] + jnp.log(den[i]), lse_tile)
        for i in range(0, len(heads), 2):
            pr = heads[i] // 2
            o_ref[0, 0, :, pr * LANES:(pr + 1) * LANES] = jnp.where(lane_lo, out[i], out[i + 1]).astype(o_ref.dtype)
    lse_ref[0, 0] = lse_tile
    if has_prev:
        kvp_ref[...] = kvc_ref[0, 0]


def _attn_prompt_group(q, kv, slopes, g):
    win, dil = DIL_GROUPS[g]
    bsz, _, n, _ = q.shape
    assert win // dil == Q_BLOCK and n % Q_BLOCK == 0
    nblk = n // Q_BLOCK
    has_prev = nblk > 1
    return pl.pallas_call(
        functools.partial(_attn_prompt_kernel, dil=dil, has_prev=has_prev),
        grid=(bsz, dil, nblk),
        in_specs=[pl.BlockSpec(memory_space=pltpu.SMEM),
                  pl.BlockSpec((1, 1, Q_BLOCK, ATT_WIDTH), lambda b, r, j: (b, r, j, 0)),
                  pl.BlockSpec((1, 1, Q_BLOCK, KV_ROW), lambda b, r, j: (b, r, j, 0))],
        out_specs=[pl.BlockSpec((1, 1, Q_BLOCK, ATT_WIDTH), lambda b, r, j: (b, r, j, 0)),
                   pl.BlockSpec((1, 1, Q_BLOCK, LANES), lambda b, r, j: (b, r, j, 0))],
        out_shape=[jax.ShapeDtypeStruct((bsz, dil, n, ATT_WIDTH), BF16),
                   jax.ShapeDtypeStruct((bsz, dil, n, LANES), F32)],
        scratch_shapes=[pltpu.VMEM((Q_BLOCK, KV_ROW), BF16)] if has_prev else [],
        compiler_params=_cparams(3),
    )(slopes, q, kv)


def _merge_out_kernel(o0_ref, o1_ref, o2_ref, l0_ref, l1_ref, l2_ref, gate_ref, e_ref, w_ref, r_ref,
                      g_ref, b_ref, out_ref, os_ref, ls_ref):
    lses, outs = [], []
    for g, (o_ref, l_ref) in enumerate(((o0_ref, l0_ref), (o1_ref, l1_ref), (o2_ref, l2_ref))):
        dil = DIL_GROUPS[g][1]
        if dil == 1:
            lses.append(l_ref[0, 0])
            outs.append(o_ref[0, 0].astype(F32))
        else:
            _interleave(ls_ref.at[g - 1], l_ref, dil)
            _interleave(os_ref.at[g - 1], o_ref, dil)
            lses.append(ls_ref[g - 1, 0])
            outs.append(_load_lane_blocked(os_ref.at[g - 1]))
    l0, l1, l2 = lses
    top = jnp.maximum(jnp.maximum(l0, l1), l2)
    w0, w1, w2 = jnp.exp(l0 - top), jnp.exp(l1 - top), jnp.exp(l2 - top)
    inv = 1.0 / (w0 + w1 + w2)
    e = e_ref[...]
    w0e = _expand_dot(w0 * inv, e)
    w1e = _expand_dot(w1 * inv, e)
    o = w0e * outs[0] + w1e * outs[1] + (1.0 - w0e - w1e) * outs[2]
    og = (o * _silu(gate_ref[...])).astype(BF16)
    v = DEEPNORM_ALPHA * r_ref[...] + _dot(og, w_ref[...])
    out_ref[...] = _layer_norm(v, g_ref[...], b_ref[...])


def _merge_out(os_, lses, gate, expand16, w, resid, g, b, seq):
    tm = PROJ_ROWS
    tpb = seq // tm
    m = resid.shape[0]
    rowblk = lambda width: pl.BlockSpec((tm, width), lambda i: (i, 0))
    const = lambda shape: pl.BlockSpec(shape, lambda i: (0, 0))
    resblk = lambda dil, width: pl.BlockSpec((1, dil, tm // dil, width), lambda i: (i // tpb, 0, i % tpb, 0))
    return pl.pallas_call(
        _merge_out_kernel,
        grid=(m // tm,),
        in_specs=[resblk(dil, ATT_WIDTH) for _, dil in DIL_GROUPS] + [resblk(dil, LANES) for _, dil in DIL_GROUPS]
                 + [rowblk(ATT_WIDTH), const((2 * LANES, ATT_WIDTH)), const((ATT_WIDTH, D_MODEL)), rowblk(D_MODEL),
                    const((1, D_MODEL)), const((1, D_MODEL))],
        out_specs=rowblk(D_MODEL),
        out_shape=jax.ShapeDtypeStruct((m, D_MODEL), F32),
        scratch_shapes=[pltpu.VMEM((N_DIL - 1, ATT_WIDTH // LANES, tm, LANES), F32),
                        pltpu.VMEM((N_DIL - 1, 1, tm, LANES), F32)],
        compiler_params=_cparams(1),
    )(*os_, *lses, gate, expand16, w, resid, g, b)


def _cache_shift_kernel(c_ref, new_ref, sel_ref, o_ref, comp_ref, *, dil):
    b = pl.program_id(0)
    x = c_ref[...]
    win = x.shape[1]
    if dil > 1:
        comp_ref[...] = _dot(x.astype(BF16), sel_ref[...]).astype(BF16)
    else:
        comp_ref[...] = x.astype(BF16)
    nv = new_ref[...]
    pick = lax.broadcasted_iota(jnp.int32, nv.shape, 1) == b
    newcol = jnp.sum(jnp.where(pick, nv, 0.0), axis=1, keepdims=True)
    last = lax.broadcasted_iota(jnp.int32, x.shape, 1) == win - 1
    o_ref[...] = jnp.where(last, newcol, pltpu.roll(x, win - 1, axis=1))


def _cache_shift(cache, kvt_new, g):
    win, dil = DIL_GROUPS[g]
    nb = cache.shape[0]
    assert cache.shape[1] == win and win // dil == Q_BLOCK
    ct = cache.transpose(0, 2, 3, 4, 1).reshape(nb * KV_ROW, win)
    rows = min(KV_ROW, SHIFT_BLOCK_BYTES // (4 * win))
    nrb = KV_ROW // rows
    sel = (jnp.arange(win)[:, None] == jnp.arange(Q_BLOCK)[None, :] * dil).astype(BF16)
    out, comp = pl.pallas_call(
        functools.partial(_cache_shift_kernel, dil=dil),
        grid=(nb, nrb),
        in_specs=[pl.BlockSpec((rows, win), lambda b, i: (b * nrb + i, 0)),
                  pl.BlockSpec((rows, nb), lambda b, i: (g * nrb + i, 0)),
                  pl.BlockSpec((win, Q_BLOCK), lambda b, i: (0, 0))],
        out_specs=[pl.BlockSpec((rows, win), lambda b, i: (b * nrb + i, 0)),
                   pl.BlockSpec((rows, Q_BLOCK), lambda b, i: (b * nrb + i, 0))],
        out_shape=[jax.ShapeDtypeStruct((nb * KV_ROW, win), cache.dtype),
                   jax.ShapeDtypeStruct((nb * KV_ROW, Q_BLOCK), BF16)],
        compiler_params=_cparams(2),
    )(ct, kvt_new, sel)
    new_cache = out.reshape(nb, 2, ATT_HEADS, ATT_HEAD_DIM, win).transpose(0, 4, 1, 2, 3)
    return new_cache, comp


def _attn_sample_kernel(slope_ref, q0_ref, q1_ref, q2_ref, gate_ref, n0_ref, n1_ref, n2_ref,
                        c0_ref, c1_ref, c2_ref, o_ref):
    nh = ATT_HEADS
    keys = Q_BLOCK
    lane = lax.broadcasted_iota(jnp.int32, (nh, ATT_WIDTH), 1)
    hrow = lax.broadcasted_iota(jnp.int32, (nh, ATT_WIDTH), 0)
    head_mask = (lane // ATT_HEAD_DIM) == hrow
    kidx = lax.broadcasted_iota(jnp.int32, (nh, keys), 1)
    slope = slope_ref[:, 0:1]
    outs, lses = [], []
    for g, (q_ref, n_ref, c_ref) in enumerate(((q0_ref, n0_ref, c0_ref), (q1_ref, n1_ref, c1_ref),
                                               (q2_ref, n2_ref, c2_ref))):
        dil = DIL_GROUPS[g][1]
        q = q_ref[0] * ATT_SCALE
        qm = jnp.where(head_mask, jnp.broadcast_to(q, (nh, ATT_WIDTH)), 0.0)
        new = n_ref[0]
        k_t = c_ref[0:ATT_WIDTH, :]
        v_t = c_ref[ATT_WIDTH:KV_ROW, :]
        dist = ((keys - kidx) * dil).astype(F32)
        s = _dot(qm.astype(BF16), k_t) - slope * dist
        s_new = jnp.sum(qm * new[:, 0:ATT_WIDTH], axis=-1, keepdims=True)
        mx = jnp.maximum(jnp.max(s, axis=-1, keepdims=True), s_new)
        p = jnp.exp(s - mx)
        p_new = jnp.exp(s_new - mx)
        den = jnp.sum(p, axis=-1, keepdims=True) + p_new
        outs.append((_dot_nt(p.astype(BF16), v_t) + p_new * new[:, ATT_WIDTH:]) / den)
        lses.append(mx + jnp.log(den))
    top = jnp.maximum(jnp.maximum(lses[0], lses[1]), lses[2])
    ws = [jnp.exp(l - top) for l in lses]
    o = (ws[0] * outs[0] + ws[1] * outs[1] + ws[2] * outs[2]) / (ws[0] + ws[1] + ws[2])
    o = jnp.sum(jnp.where(head_mask, o, 0.0), axis=0, keepdims=True)
    o_ref[0] = o * _silu(gate_ref[0])


def _attn_sample(proj, kv_new, comps, slopes_b):
    nb = proj.shape[0]
    proj3 = proj.reshape(nb, 1, 4 * ATT_WIDTH)
    kvn3 = kv_new.reshape(nb, 1, N_DIL * KV_ROW)
    qspec = lambda g: pl.BlockSpec((1, 1, ATT_WIDTH), lambda b: (b, 0, g))
    nspec = lambda g: pl.BlockSpec((1, 1, KV_ROW), lambda b: (b, 0, g))
    cspec = pl.BlockSpec((KV_ROW, Q_BLOCK), lambda b: (b, 0))
    o = pl.pallas_call(
        _attn_sample_kernel,
        grid=(nb,),
        in_specs=[pl.BlockSpec((ATT_HEADS, LANES), lambda b: (0, 0)),
                  qspec(0), qspec(1), qspec(2), qspec(3), nspec(0), nspec(1), nspec(2),
                  cspec, cspec, cspec],
        out_specs=pl.BlockSpec((1, 1, ATT_WIDTH), lambda b: (b, 0, 0)),
        out_shape=jax.ShapeDtypeStruct((nb, 1, ATT_WIDTH), F32),
        compiler_params=_cparams(1),
    )(slopes_b, proj3, proj3, proj3, proj3, kvn3, kvn3, kvn3, *comps)
    return o.reshape(nb, ATT_WIDTH)


def _pad_lanes(v):
    return jnp.pad(v.astype(F32), (0, LANES - v.shape[0])).reshape(1, LANES)


def kernel(x_prompt, x_sample, state_ssm, state_conv, cache_kv_w128, cache_kv_w512, cache_kv_w2048,
           a_in_proj, a_conv_w, a_conv_b, a_dt_bias, a_log, a_d, a_norm_w, a_out_proj,
           kv_proj, b_in_proj, b_out_proj, ln_g, ln_b):
    bsz, seq, _ = x_prompt.shape
    nb = x_sample.shape[0]
    assert x_sample.shape[1] == 1 and seq % CHUNK == 0
    caches = (cache_kv_w128, cache_kv_w512, cache_kv_w2048)

    heads = jnp.arange(LANES)[:, None]
    expand32 = (heads == jnp.arange(D_INNER)[None, :] // SSM_HEAD_DIM).astype(BF16)
    expand16 = (heads == jnp.arange(ATT_WIDTH)[None, :] // ATT_HEAD_DIM).astype(BF16)
    expand32 = jnp.concatenate([expand32, expand32], axis=0)
    expand16 = jnp.concatenate([expand16, expand16], axis=0)
    tril = (jnp.arange(CHUNK)[:, None] >= jnp.arange(CHUNK)[None, :]).astype(BF16)
    eye = jnp.eye(LANES, dtype=BF16)
    tap = jnp.arange(CONV_WIDTH * CHUNK)
    shift = (jnp.arange(2 * CHUNK)[None, :]
             == (CHUNK + tap % CHUNK - (CONV_WIDTH - 1 - tap // CHUNK))[:, None]).astype(BF16)
    slopes =jnp.exp2(-8.0 * jnp.arange(1, ATT_HEADS + 1, dtype=F32) / ATT_HEADS)
    slopes_b = jnp.broadcast_to(slopes[:, None], (ATT_HEADS, LANES))

    hp = x_prompt.reshape(bsz * seq, D_MODEL)
    hs = x_sample.reshape(nb, D_MODEL)
    stacks_p, stacks_s = None, None

    for i in range(N_A_LAYERS):
        w_in = a_in_proj[i]
        w_main = w_in[:, 0:D_INNER + CONV_DIM].astype(BF16)
        w_dt = jnp.pad(w_in[:, D_INNER + CONV_DIM:], ((0, 0), (0, LANES - SSM_HEADS))).astype(BF16)
        w_out = a_out_proj[i].astype(BF16)
        prm = dict(
            cwx=a_conv_w[i][:, 0:D_INNER], cwbc=a_conv_w[i][:, D_INNER:],
            cbx=a_conv_b[i][0:D_INNER].reshape(1, D_INNER), cbbc=a_conv_b[i][D_INNER:].reshape(1, BC_WIDTH),
            dtb=_pad_lanes(a_dt_bias[i]), alog=_pad_lanes(a_log[i]),
            dexp=jnp.repeat(a_d[i].astype(F32), SSM_HEAD_DIM).reshape(1, D_INNER),
            nw=a_norm_w[i].reshape(1, D_INNER), expand2=expand32, tril=tril, eye=eye, shift=shift)
        g_ln, b_ln = ln_g[i].reshape(1, D_MODEL), ln_b[i].reshape(1, D_MODEL)

        zx, dtp = _a_in_proj(hp, w_main, w_dt, BF16)
        yn, *stacks_p = _ssd_prompt(zx, dtp, prm, bsz, seq, i, stacks_p)
        hp = _matmul_ln(yn, w_out, hp, g_ln, b_ln)

        zx_s, dt_s = _a_in_proj(hs, w_main, w_dt, F32)
        yn_s, *stacks_s = _ssd_sample(zx_s, dt_s, state_conv, state_ssm, prm, i, stacks_s)
        hs = _matmul_ln(yn_s, w_out, hs, g_ln, b_ln)
    ssm_p, conv_p = stacks_p
    conv_s, ssm_s = stacks_s

    kvw = kv_proj.reshape(D_MODEL, 2, N_DIL, ATT_WIDTH)
    kvw_g = [jnp.concatenate([kvw[:, 0, g], kvw[:, 1, g]], axis=1).astype(BF16) for g in range(N_DIL)]
    kvw_all = jnp.concatenate(kvw_g, axis=1)
    kv_p, new_kv_p = [], []
    for g in range(N_DIL):
        kv_res, kv_t = _kv_proj(hp, kvw_g[g], g, bsz, seq)
        kv_p.append(kv_res)
        new_kv_p.append(kv_t.reshape(bsz, 2, ATT_HEADS, ATT_HEAD_DIM, kv_t.shape[-1]).transpose(0, 4, 1, 2, 3))
    kv_s = _matmul(hs, kvw_all, tn=1024)
    kvt_s = _matmul_nt(kvw_all.T, hs, tn=1024)
    new_kv_s, comps = [], []
    for g in range(N_DIL):
        shifted, comp = _cache_shift(caches[g], kvt_s, g)
        new_kv_s.append(shifted)
        comps.append(comp)

    for j in range(N_B_LAYERS):
        layer = N_A_LAYERS + j
        w_in = b_in_proj[j].astype(BF16)
        w_out = b_out_proj[j].astype(BF16)
        g_ln, b_ln = ln_g[layer].reshape(1, D_MODEL), ln_b[layer].reshape(1, D_MODEL)

        q0, q1, q2, gate = _b_in_proj(hp, w_in, bsz, seq)
        res = [_attn_prompt_group(q, kv_p[g], slopes, g) for g, q in enumerate((q0, q1, q2))]
        hp = _merge_out([r[0] for r in res], [r[1] for r in res], gate, expand16, w_out, hp, g_ln, b_ln, seq)

        proj_s = _matmul(hs, w_in, tn=1024)
        og_s = _attn_sample(proj_s, kv_s, comps, slopes_b)
        hs = _matmul_ln(og_s, w_out, hs, g_ln, b_ln)

    return (hp.reshape(bsz, seq, D_MODEL), hs.reshape(nb, 1, D_MODEL),
            ssm_p, conv_p, new_kv_p[0], new_kv_p[1], new_kv_p[2],
            ssm_s, conv_s, new_kv_s[0], new_kv_s[1], new_kv_s[2])
```

```python
import functools

import jax
import jax.numpy as jnp
from jax import lax
from jax.experimental import pallas as pl
from jax.experimental.pallas import tpu as pltpu

F32 = jnp.float32
BF16 = jnp.bfloat16

D_MODEL = 1024
N_A_LAYERS = 2
N_B_LAYERS = 2
D_INNER = 2048
SSM_HEAD_DIM = 64
SSM_HEADS = 32
SSM_GROUPS = 4
SSM_STATE = 128
HEADS_PER_GROUP = SSM_HEADS // SSM_GROUPS
GROUP_WIDTH = D_INNER // SSM_GROUPS
CONV_WIDTH = 4
BC_WIDTH = 2 * SSM_GROUPS * SSM_STATE
CONV_DIM = D_INNER + BC_WIDTH
CHUNK = 128
DIL_GROUPS = ((128, 1), (512, 4), (2048, 16))
N_DIL = 3
ATT_HEADS = 16
ATT_HEAD_DIM = 64
ATT_WIDTH = 1024
KV_ROW = 2 * ATT_WIDTH
Q_BLOCK = 128
ATT_SCALE = ATT_HEAD_DIM ** -0.5
LN_EPS = 1e-5
RMS_EPS = 1e-5
DEEPNORM_ALPHA = (2.0 * 4) ** 0.25

LANES = 128
SUBLANES = 8
VMEM_LIMIT = 48 * 1024 * 1024

ATT_UNITS = 2
ATT_HEAD_BATCH = 16
PROJ_ROWS = 512
PROJ_COLS = 512
MM_COLS = 256
KV_ROWS = 512
SHIFT_BLOCK_BYTES = 4 * 1024 * 1024


def _cparams(n_grid):
    return pltpu.CompilerParams(dimension_semantics=("arbitrary",) * n_grid, vmem_limit_bytes=VMEM_LIMIT)


def _silu(x):
    h = 0.5 * x
    return h + h * jnp.tanh(h)


def _softplus(x):
    return jnp.maximum(x, 0.0) + jnp.log1p(jnp.exp(-jnp.abs(x)))


def _split3(a):
    hi = a.astype(BF16)
    r1 = a - hi.astype(F32)
    mid = r1.astype(BF16)
    lo = (r1 - mid.astype(F32)).astype(BF16)
    return hi, mid, lo


def _dot(a, b):
    return jnp.dot(a, b, preferred_element_type=F32)


def _dot_nt(a, b):
    return lax.dot_general(a, b, (((1,), (1,)), ((), ())), preferred_element_type=F32)


def _expand_dot(a, m01_twice):
    hi = a.astype(BF16)
    lo = (a - hi.astype(F32)).astype(BF16)
    return _dot(jnp.concatenate([hi, lo], axis=1), m01_twice)


def _exact_dot_left(m01, a):
    hi, mid, lo = _split3(a)
    return _dot(m01, hi) + _dot(m01, mid) + _dot(m01, lo)


def _layer_norm(v, g, b):
    mu = jnp.mean(v, axis=-1, keepdims=True)
    d = v - mu
    var = jnp.mean(d * d, axis=-1, keepdims=True)
    return d * lax.rsqrt(var + LN_EPS) * g + b


def _lane_blocks(width):
    return [slice(cb * LANES, (cb + 1) * LANES) for cb in range(width // LANES)]


def _deinterleave(dst_ref, blk_ref, dil):
    rows = blk_ref.shape[1] // dil
    for r in range(dil):
        for cb, sl in enumerate(_lane_blocks(dst_ref.shape[-1])):
            dst_ref[0, r, :, sl] = blk_ref[cb, pl.ds(r, rows, stride=dil), :].astype(dst_ref.dtype)


def _interleave(blk_ref, src_ref, dil):
    rows = blk_ref.shape[1] // dil
    for r in range(dil):
        for cb, sl in enumerate(_lane_blocks(src_ref.shape[-1])):
            blk_ref[cb, pl.ds(r, rows, stride=dil), :] = src_ref[0, r, :, sl].astype(blk_ref.dtype)


def _mm_kernel(x_ref, w_ref, o_ref, xb_ref):
    @pl.when(pl.program_id(1) == 0)
    def _():
        xb_ref[...] = x_ref[...].astype(BF16)

    o_ref[...] = _dot(xb_ref[...], w_ref[...]).astype(o_ref.dtype)


def _matmul(x, w, *, tn, out_dtype=F32, tm=1024):
    m, k = x.shape
    n = w.shape[1]
    tm = min(tm, m)
    assert m % tm == 0 and n % tn == 0
    return pl.pallas_call(
        _mm_kernel,
        grid=(m // tm, n // tn),
        in_specs=[pl.BlockSpec((tm, k), lambda i, j: (i, 0)),
                  pl.BlockSpec((k, tn), lambda i, j: (0, j))],
        out_specs=pl.BlockSpec((tm, tn), lambda i, j: (i, j)),
        out_shape=jax.ShapeDtypeStruct((m, n), out_dtype),
        scratch_shapes=[pltpu.VMEM((tm, k), BF16)],
        compiler_params=_cparams(2),
    )(x, w)


def _a_in_proj_kernel(x_ref, w_ref, wdt_ref, zx_ref, dt_ref):
    xb = x_ref[...].astype(BF16)
    for c in range(w_ref.shape[1] // PROJ_COLS):
        sl = slice(c * PROJ_COLS, (c + 1) * PROJ_COLS)
        zx_ref[:, sl] = _dot(xb, w_ref[:, sl]).astype(zx_ref.dtype)
    dt_ref[...] = _dot(xb, wdt_ref[...])


def _a_in_proj(x, w_main, w_dt, out_dtype):
    m, k = x.shape
    n = w_main.shape[1]
    tm = min(PROJ_ROWS, m)
    assert m % tm == 0 and n % PROJ_COLS == 0
    return pl.pallas_call(
        _a_in_proj_kernel,
        grid=(m // tm,),
        in_specs=[pl.BlockSpec((tm, k), lambda i: (i, 0)),
                  pl.BlockSpec((k, n), lambda i: (0, 0)),
                  pl.BlockSpec((k, LANES), lambda i: (0, 0))],
        out_specs=[pl.BlockSpec((tm, n), lambda i: (i, 0)),
                   pl.BlockSpec((tm, LANES), lambda i: (i, 0))],
        out_shape=[jax.ShapeDtypeStruct((m, n), out_dtype),
                   jax.ShapeDtypeStruct((m, LANES), F32)],
        compiler_params=_cparams(1),
    )(x, w_main, w_dt)


def _mm_nt_kernel(w_ref, x_ref, o_ref):
    o_ref[...] = _dot_nt(w_ref[...], x_ref[...].astype(BF16))


def _matmul_nt(w_t, x, *, tn):
    n, k = w_t.shape
    m = x.shape[0]
    assert n % tn == 0
    return pl.pallas_call(
        _mm_nt_kernel,
        grid=(n // tn,),
        in_specs=[pl.BlockSpec((tn, k), lambda i: (i, 0)),
                  pl.BlockSpec((m, k), lambda i: (0, 0))],
        out_specs=pl.BlockSpec((tn, m), lambda i: (i, 0)),
        out_shape=jax.ShapeDtypeStruct((n, m), F32),
        compiler_params=_cparams(1),
    )(w_t, x)


def _mm_ln_kernel(y_ref, w_ref, r_ref, g_ref, b_ref, o_ref):
    acc = _dot(y_ref[...].astype(BF16), w_ref[...])
    v = DEEPNORM_ALPHA * r_ref[...] + acc
    o_ref[...] = _layer_norm(v, g_ref[...], b_ref[...])


def _matmul_ln(y, w, resid, g, b, *, tm=512):
    m, k = y.shape
    n = w.shape[1]
    tm = min(tm, m)
    assert m % tm == 0
    return pl.pallas_call(
        _mm_ln_kernel,
        grid=(m // tm,),
        in_specs=[pl.BlockSpec((tm, k), lambda i: (i, 0)),
                  pl.BlockSpec((k, n), lambda i: (0, 0)),
                  pl.BlockSpec((tm, n), lambda i: (i, 0)),
                  pl.BlockSpec((1, n), lambda i: (0, 0)),
                  pl.BlockSpec((1, n), lambda i: (0, 0))],
        out_specs=pl.BlockSpec((tm, n), lambda i: (i, 0)),
        out_shape=jax.ShapeDtypeStruct((m, n), F32),
        compiler_params=_cparams(1),
    )(y, w, resid, g, b)


def _ssd_prompt_kernel(z_ref, xr_ref, bcr_ref, dtr_ref, cwx_ref, cwbc_ref, cbx_ref, cbbc_ref,
                       dtb_ref, alog_ref, dexp_ref, nw_ref, e_ref, tril_ref, shift_ref,
                       y_ref, ssm_ref, conv_ref,
                       st_ref, extx_ref, extbc_ref, lhs_ref, bc_ref, xs_ref, xdtb_ref, xwb_ref, eae_ref,
                       cde_ref, yp_ref):
    c = pl.program_id(1)
    t = CHUNK
    wide = 2 * LANES
    conv_ref = conv_ref.at[0, 0]

    @pl.when(c == 0)
    def _():
        st_ref[...] = jnp.zeros_like(st_ref)
        extx_ref[0:t, :] = jnp.zeros((t, D_INNER), BF16)
        extbc_ref[0:t, :] = jnp.zeros((t, BC_WIDTH), BF16)

    dt = _softplus(dtr_ref[...] + dtb_ref[...])
    a = -jnp.exp(alog_ref[...])
    acs = _exact_dot_left(tril_ref[...], dt * a)
    acs_t = acs.T
    a_last = acs[t - 1:t, :]
    stacked = jnp.concatenate(
        [dt, jnp.exp(acs), jnp.exp(a_last - acs), jnp.broadcast_to(jnp.exp(a_last), (2 * SUBLANES, LANES))], axis=0)
    hi = stacked.astype(BF16)
    lhs_ref[:, 0:LANES] = hi
    lhs_ref[:, LANES:wide] = (stacked - hi.astype(F32)).astype(BF16)

    extx_ref[t:2 * t, :] = xr_ref[...]
    extbc_ref[t:2 * t, :] = bcr_ref[...]

    def conv(ext_ref, w_ref, b_ref, sl):
        taps = _dot(shift_ref[...], ext_ref[:, sl]).reshape(t // SUBLANES, CONV_WIDTH, SUBLANES, wide)
        acc = jnp.broadcast_to(b_ref[:, sl], (SUBLANES, wide))
        for k in range(CONV_WIDTH):
            acc = acc + taps[:, k] * w_ref[k:k + 1, sl]
        tail = taps[t // SUBLANES - 1, CONV_WIDTH - 1, SUBLANES - (CONV_WIDTH - 1):SUBLANES]
        return acc.reshape(t, wide), tail

    for cb in range(BC_WIDTH // wide):
        sl = slice(cb * wide, (cb + 1) * wide)
        acc, tail = conv(extbc_ref, cwbc_ref, cbbc_ref, sl)
        conv_ref[:, D_INNER + cb * wide:D_INNER + (cb + 1) * wide] = tail
        bc_ref[:, sl] = _silu(acc)
    for cb in range(D_INNER // wide):
        sl = slice(cb * wide, (cb + 1) * wide)
        acc, tail = conv(extx_ref, cwx_ref, cbx_ref, sl)
        conv_ref[:, sl] = tail
        xs = _silu(acc)
        ex = _dot(lhs_ref[...], e_ref[:, sl])
        xdt = xs * ex[0:t]
        xs_ref[:, sl] = xs
        xdtb_ref[:, sl] = xdt.astype(BF16)
        xwb_ref[:, sl] = (xdt * ex[2 * t:3 * t]).astype(BF16)
        eae_ref[:, sl] = ex[t:2 * t]
        cde_ref[:, sl] = ex[3 * t:3 * t + SUBLANES]
    extx_ref[0:t, :] = xr_ref[...]
    extbc_ref[0:t, :] = bcr_ref[...]

    row = lax.broadcasted_iota(jnp.int32, (t, t), 0)
    col = lax.broadcasted_iota(jnp.int32, (t, t), 1)
    causal = row >= col
    lane_lo = lax.broadcasted_iota(jnp.int32, (t, LANES), 1) < SSM_HEAD_DIM

    for g in range(SSM_GROUPS):
        g0 = g * GROUP_WIDTH
        gsl = slice(g0, g0 + GROUP_WIDTH)
        bg = bc_ref[:, g * SSM_STATE:(g + 1) * SSM_STATE]
        cg_b = bc_ref[:, (SSM_GROUPS + g) * SSM_STATE:(SSM_GROUPS + g + 1) * SSM_STATE].astype(BF16)
        cb = _dot_nt(cg_b, bg.astype(BF16))
        s_old = st_ref[:, gsl]
        yp_ref[:, gsl] = _dot(cg_b, s_old.astype(BF16)) * eae_ref[:, gsl]
        states = _dot(bg.T.astype(BF16), xwb_ref[:, gsl])
        st_ref[:, gsl] = s_old * cde_ref[0:1, gsl] + states
        sq = jnp.zeros((t, LANES), F32)
        for pr in range(HEADS_PER_GROUP // 2):
            sl = slice(g0 + pr * LANES, g0 + (pr + 1) * LANES)
            xp = xdtb_ref[:, sl]
            halves = []
            for half in range(2):
                h = g * HEADS_PER_GROUP + 2 * pr + half
                seg = acs[:, h:h + 1] - acs_t[h:h + 1, :]
                dec = jnp.exp(jnp.where(causal, seg, -jnp.inf))
                halves.append(_dot((cb * dec).astype(BF16), xp))
            y = jnp.where(lane_lo, halves[0], halves[1]) + yp_ref[:, sl] + xs_ref[:, sl] * dexp_ref[:, sl]
            hz = y * _silu(z_ref[:, sl].astype(F32))
            yp_ref[:, sl] = hz
            sq = sq + hz * hz
        scale = lax.rsqrt(jnp.sum(sq, axis=-1, keepdims=True) * (1.0 / GROUP_WIDTH) + RMS_EPS)
        y_ref[:, gsl] = (yp_ref[:, gsl] * scale * nw_ref[:, gsl]).astype(y_ref.dtype)

    @pl.when(c == pl.num_programs(1) - 1)
    def _():
        for j in range(D_INNER // LANES):
            tile = st_ref[:, j * LANES:(j + 1) * LANES].T
            ssm_ref[0, 0, 2 * j:2 * j + 2] = tile.reshape(2, SSM_HEAD_DIM, SSM_STATE)


def _skip_carried(body, n_in, n_carry):
    def kern(*refs):
        body(*refs[:n_in], *refs[n_in + n_carry:])
    return kern


def _ssd_prompt(zx, dtp, prm, bsz, seq, layer, carried):
    nc = seq // CHUNK
    m = bsz * seq
    row = lambda b, c: b * nc + c
    const = lambda shape: pl.BlockSpec(shape, lambda b, c: (0,) * len(shape))
    ins = [zx, zx, zx, dtp, prm["cwx"], prm["cwbc"], prm["cbx"], prm["cbbc"], prm["dtb"], prm["alog"],
           prm["dexp"], prm["nw"], prm["expand2"], prm["tril"], prm["shift"]]
    extra = [] if carried is None else list(carried)
    aliases = {} if carried is None else {len(ins): 1, len(ins) + 1: 2}
    return pl.pallas_call(
        _skip_carried(_ssd_prompt_kernel, len(ins), len(extra)),
        grid=(bsz, nc),
        in_specs=[pl.BlockSpec((CHUNK, D_INNER), lambda b, c: (row(b, c), 0)),
                  pl.BlockSpec((CHUNK, D_INNER), lambda b, c: (row(b, c), 1)),
                  pl.BlockSpec((CHUNK, BC_WIDTH), lambda b, c: (row(b, c), 4)),
                  pl.BlockSpec((CHUNK, LANES), lambda b, c: (row(b, c), 0)),
                  const((CONV_WIDTH, D_INNER)), const((CONV_WIDTH, BC_WIDTH)),
                  const((1, D_INNER)), const((1, BC_WIDTH)),
                  const((1, LANES)), const((1, LANES)),
                  const((1, D_INNER)), const((1, D_INNER)),
                  const((2 * LANES, D_INNER)), const((CHUNK, CHUNK)), const((CONV_WIDTH * CHUNK, 2 * CHUNK))]
                 + [pl.BlockSpec(memory_space=pl.ANY)] * len(extra),
        out_specs=[pl.BlockSpec((CHUNK, D_INNER), lambda b, c: (row(b, c), 0)),
                   pl.BlockSpec((1, 1, SSM_HEADS, SSM_HEAD_DIM, SSM_STATE), lambda b, c: (layer, b, 0, 0, 0)),
                   pl.BlockSpec((1, 1, CONV_WIDTH - 1, CONV_DIM), lambda b, c: (layer, b, 0, 0))],
        out_shape=[jax.ShapeDtypeStruct((m, D_INNER), BF16),
                   jax.ShapeDtypeStruct((N_A_LAYERS, bsz, SSM_HEADS, SSM_HEAD_DIM, SSM_STATE), F32),
                   jax.ShapeDtypeStruct((N_A_LAYERS, bsz, CONV_WIDTH - 1, CONV_DIM), F32)],
        scratch_shapes=[pltpu.VMEM((SSM_STATE, D_INNER), F32),
                        pltpu.VMEM((2 * CHUNK, D_INNER), BF16),
                        pltpu.VMEM((2 * CHUNK, BC_WIDTH), BF16),
                        pltpu.VMEM((3 * CHUNK + 2 * SUBLANES, 2 * LANES), BF16),
                        pltpu.VMEM((CHUNK, BC_WIDTH), F32),
                        pltpu.VMEM((CHUNK, D_INNER), F32),
                        pltpu.VMEM((CHUNK, D_INNER), BF16),
                        pltpu.VMEM((CHUNK, D_INNER), BF16),
                        pltpu.VMEM((CHUNK, D_INNER), F32),
                        pltpu.VMEM((SUBLANES, D_INNER), F32),
                        pltpu.VMEM((CHUNK, D_INNER), F32)],
        input_output_aliases=aliases,
        compiler_params=_cparams(2),
    )(*ins, *extra)


def _ssd_sample_kernel(z_ref, xr_ref, bcr_ref, dtr_ref, conv_ref, ssm_ref,
                       cwx_ref, cwbc_ref, cbx_ref, cbbc_ref, dtb_ref, alog_ref, dexp_ref, nw_ref,
                       e_ref, eye_ref,
                       y_ref, convo_ref, ssmo_ref, st_ref):
    prev = conv_ref[0, 0]
    raw = jnp.concatenate([xr_ref[0], bcr_ref[0]], axis=1)
    cw = jnp.concatenate([cwx_ref[...], cwbc_ref[...]], axis=1)
    cbias = jnp.concatenate([cbx_ref[...], cbbc_ref[...]], axis=1)
    acc = cbias + raw * cw[CONV_WIDTH - 1:CONV_WIDTH, :]
    for k in range(CONV_WIDTH - 1):
        acc = acc + prev[k:k + 1, :] * cw[k:k + 1, :]
    convo_ref[0, 0, 0:CONV_WIDTH - 2, :] = prev[1:CONV_WIDTH - 1, :]
    convo_ref[0, 0, CONV_WIDTH - 2:CONV_WIDTH - 1, :] = raw
    xbc = _silu(acc)
    xs = xbc[:, 0:D_INNER]
    bm = xbc[:, D_INNER:D_INNER + SSM_GROUPS * SSM_STATE]
    cm = xbc[:, D_INNER + SSM_GROUPS * SSM_STATE:]

    dt = _softplus(dtr_ref[0] + dtb_ref[...])
    dec = jnp.exp(dt * -jnp.exp(alog_ref[...]))
    stacked = jnp.concatenate([dt, dec, jnp.zeros((SUBLANES - 2, LANES), F32)], axis=0)
    ex = _expand_dot(stacked, e_ref[...])
    dt_e = ex[0:1]
    dec_e = ex[1:2]
    xdt = xs * dt_e

    rows = jnp.concatenate([bm[:, g * SSM_STATE:(g + 1) * SSM_STATE] for g in range(SSM_GROUPS)]
                           + [cm[:, g * SSM_STATE:(g + 1) * SSM_STATE] for g in range(SSM_GROUPS)], axis=0)
    hi, mid, lo = _split3(rows)
    eye = eye_ref[...]
    cols = _dot_nt(eye, hi) + _dot_nt(eye, mid) + _dot_nt(eye, lo)

    for j in range(D_INNER // LANES):
        tile = ssm_ref[0, 0, 2 * j:2 * j + 2].reshape(LANES, SSM_STATE)
        st_ref[:, j * LANES:(j + 1) * LANES] = tile.T
    y_groups = []
    for g in range(SSM_GROUPS):
        g0 = g * GROUP_WIDTH
        new = (st_ref[:, g0:g0 + GROUP_WIDTH] * dec_e[:, g0:g0 + GROUP_WIDTH]
               + cols[:, g:g + 1] * xdt[:, g0:g0 + GROUP_WIDTH])
        st_ref[:, g0:g0 + GROUP_WIDTH] = new
        y_groups.append(jnp.sum(new * cols[:, SSM_GROUPS + g:SSM_GROUPS + g + 1], axis=0, keepdims=True))
    for j in range(D_INNER // LANES):
        tile = st_ref[:, j * LANES:(j + 1) * LANES].T
        ssmo_ref[0, 0, 2 * j:2 * j + 2] = tile.reshape(2, SSM_HEAD_DIM, SSM_STATE)
    y = jnp.concatenate(y_groups, axis=1) + xs * dexp_ref[...]

    hz = y * _silu(z_ref[0])
    normed = []
    for g in range(SSM_GROUPS):
        hg = hz[:, g * GROUP_WIDTH:(g + 1) * GROUP_WIDTH]
        normed.append(hg * lax.rsqrt(jnp.mean(hg * hg, axis=-1, keepdims=True) + RMS_EPS))
    y_ref[0] = jnp.concatenate(normed, axis=1) * nw_ref[...]


def _ssd_sample(zx, dts, conv_state, ssm_state, prm, layer, carried):
    nb = zx.shape[0]
    zx3 = zx.reshape(nb, 1, zx.shape[1])
    dt3 = dts.reshape(nb, 1, LANES)
    const = lambda shape: pl.BlockSpec(shape, lambda b: (0,) * len(shape))
    conv_blk = pl.BlockSpec((1, 1, CONV_WIDTH - 1, CONV_DIM), lambda b: (layer, b, 0, 0))
    ssm_blk = pl.BlockSpec((1, 1, SSM_HEADS, SSM_HEAD_DIM, SSM_STATE), lambda b: (layer, b, 0, 0, 0))
    ins = [zx3, zx3, zx3, dt3, conv_state, ssm_state, prm["cwx"], prm["cwbc"], prm["cbx"], prm["cbbc"],
           prm["dtb"], prm["alog"], prm["dexp"], prm["nw"], prm["expand2"], prm["eye"]]
    extra = [] if carried is None else list(carried)
    aliases = {} if carried is None else {len(ins): 1, len(ins) + 1: 2}
    y, conv_new, ssm_new = pl.pallas_call(
        _skip_carried(_ssd_sample_kernel, len(ins), len(extra)),
        grid=(nb,),
        in_specs=[pl.BlockSpec((1, 1, D_INNER), lambda b: (b, 0, 0)),
                  pl.BlockSpec((1, 1, D_INNER), lambda b: (b, 0, 1)),
                  pl.BlockSpec((1, 1, BC_WIDTH), lambda b: (b, 0, 4)),
                  pl.BlockSpec((1, 1, LANES), lambda b: (b, 0, 0)),
                  conv_blk, ssm_blk,
                  const((CONV_WIDTH, D_INNER)), const((CONV_WIDTH, BC_WIDTH)),
                  const((1, D_INNER)), const((1, BC_WIDTH)),
                  const((1, LANES)), const((1, LANES)),
                  const((1, D_INNER)), const((1, D_INNER)),
                  const((2 * LANES, D_INNER)), const((LANES, LANES))]
                 + [pl.BlockSpec(memory_space=pl.ANY)] * len(extra),
        out_specs=[pl.BlockSpec((1, 1, D_INNER), lambda b: (b, 0, 0)), conv_blk, ssm_blk],
        out_shape=[jax.ShapeDtypeStruct((nb, 1, D_INNER), F32),
                   jax.ShapeDtypeStruct(conv_state.shape, F32),
                   jax.ShapeDtypeStruct(ssm_state.shape, F32)],
        scratch_shapes=[pltpu.VMEM((SSM_STATE, D_INNER), F32)],
        input_output_aliases=aliases,
        compiler_params=_cparams(1),
    )(*ins, *extra)
    return y.reshape(nb, D_INNER), conv_new, ssm_new


def _b_in_proj_kernel(x_ref, w_ref, q0_ref, q1_ref, q2_ref, gate_ref, acc_ref):
    xb = x_ref[...].astype(BF16)
    for g, dst in enumerate((q0_ref, q1_ref, q2_ref)):
        dil = DIL_GROUPS[g][1]
        for c0 in range(0, ATT_WIDTH, MM_COLS):
            acc = _dot(xb, w_ref[:, g * ATT_WIDTH + c0:g * ATT_WIDTH + c0 + MM_COLS])
            if dil == 1:
                dst[0, 0, :, c0:c0 + MM_COLS] = acc.astype(dst.dtype)
            else:
                for cb in range(MM_COLS // LANES):
                    acc_ref[c0 // LANES + cb] = acc[:, cb * LANES:(cb + 1) * LANES]
        if dil > 1:
            _deinterleave(dst, acc_ref, dil)
    for c0 in range(0, ATT_WIDTH, PROJ_COLS):
        gate_ref[:, c0:c0 + PROJ_COLS] = _dot(xb, w_ref[:, N_DIL * ATT_WIDTH + c0:N_DIL * ATT_WIDTH + c0 + PROJ_COLS])


def _b_in_proj(x, w, bsz, seq):
    tm = PROJ_ROWS
    tpb = seq // tm
    m, k = x.shape
    qspecs, qshapes = [], []
    for _, dil in DIL_GROUPS:
        assert tm % (dil * 2 * SUBLANES) == 0
        qspecs.append(pl.BlockSpec((1, dil, tm // dil, ATT_WIDTH), lambda i: (i // tpb, 0, i % tpb, 0)))
        qshapes.append(jax.ShapeDtypeStruct((bsz, dil, seq // dil, ATT_WIDTH), BF16))
    return pl.pallas_call(
        _b_in_proj_kernel,
        grid=(m // tm,),
        in_specs=[pl.BlockSpec((tm, k), lambda i: (i, 0)),
                  pl.BlockSpec(w.shape, lambda i: (0, 0))],
        out_specs=qspecs + [pl.BlockSpec((tm, ATT_WIDTH), lambda i: (i, 0))],
        out_shape=qshapes + [jax.ShapeDtypeStruct((m, ATT_WIDTH), F32)],
        scratch_shapes=[pltpu.VMEM((ATT_WIDTH // LANES, tm, LANES), F32)],
        compiler_params=_cparams(1),
    )(x, w)


def _kv_proj_kernel(x_ref, w_ref, kv_ref, kvt_ref, acc_ref, *, dil, first_tile, wt):
    t = pl.program_id(1)
    xb = x_ref[...].astype(BF16)
    for c0 in range(0, KV_ROW, MM_COLS):
        acc = _dot(xb, w_ref[:, c0:c0 + MM_COLS])
        for cb in range(MM_COLS // LANES):
            acc_ref[c0 // LANES + cb] = acc[:, cb * LANES:(cb + 1) * LANES]
        if dil == 1:
            kv_ref[0, 0, :, c0:c0 + MM_COLS] = acc.astype(kv_ref.dtype)
    if dil > 1:
        _deinterleave(kv_ref, acc_ref, dil)

    @pl.when(t >= first_tile)
    def _():
        tm = acc_ref.shape[1]
        for cb in range(KV_ROW // LANES):
            for rb in range(wt // LANES):
                r0 = tm - wt + rb * LANES
                kvt_ref[0, cb * LANES:(cb + 1) * LANES, rb * LANES:(rb + 1) * LANES] = (
                    acc_ref[cb, r0:r0 + LANES, :].T)


def _kv_proj(x, w, g, bsz, seq):
    win, dil = DIL_GROUPS[g]
    win = min(win, seq)
    tm = KV_ROWS
    tpb = seq // tm
    wt = min(win, tm)
    first_tile = tpb - win // wt
    assert tm % (dil * 2 * SUBLANES) == 0 and win % wt == 0 and seq % tm == 0
    k = x.shape[1]
    return pl.pallas_call(
        functools.partial(_kv_proj_kernel, dil=dil, first_tile=first_tile, wt=wt),
        grid=(bsz, tpb),
        in_specs=[pl.BlockSpec((tm, k), lambda b, t: (b * tpb + t, 0)),
                  pl.BlockSpec((k, KV_ROW), lambda b, t: (0, 0))],
        out_specs=[pl.BlockSpec((1, dil, tm // dil, KV_ROW), lambda b, t: (b, 0, t, 0)),
                   pl.BlockSpec((1, KV_ROW, wt), lambda b, t: (b, 0, jnp.maximum(t - first_tile, 0)))],
        out_shape=[jax.ShapeDtypeStruct((bsz, dil, seq // dil, KV_ROW), BF16),
                   jax.ShapeDtypeStruct((bsz, KV_ROW, win), F32)],
        scratch_shapes=[pltpu.VMEM((KV_ROW // LANES, tm, LANES), F32)],
        compiler_params=_cparams(2),
    )(x, w)


def _attn_prompt_kernel(slope_ref, q_ref, kv_ref, o_ref, lse_ref, *scratch, dil, has_prev, units):
    if has_prev:
        j = pl.program_id(2)
        kvp_ref, = scratch

        @pl.when(j == 0)
        def _():
            kvp_ref[...] = jnp.zeros_like(kvp_ref)

        rows = [pl.ds(u * Q_BLOCK, Q_BLOCK) for u in range(units)]
        for u in range(units):
            prev = kvp_ref if u == 0 else kv_ref.at[0, 0, rows[u - 1]]
            _attn_block(slope_ref, q_ref.at[0, 0, rows[u]], kv_ref.at[0, 0, rows[u]], prev,
                        o_ref.at[0, 0, rows[u]], lse_ref.at[0, 0, rows[u]], (j == 0) if u == 0 else None, dil)
        kvp_ref[...] = kv_ref[0, 0, rows[units - 1]]
    else:
        for u in range(units):
            _attn_block(slope_ref, q_ref.at[0, u], kv_ref.at[0, u], None, o_ref.at[0, u], lse_ref.at[0, u], None, dil)


def _attn_block(slope_ref, q_ref, kvc_ref, kvp_ref, o_ref, lse_ref, first, dil):
    has_prev = kvp_ref is not None
    qb = Q_BLOCK
    q = (q_ref[...].astype(F32) * ATT_SCALE).astype(BF16)
    row = lax.broadcasted_iota(jnp.int32, (qb, qb), 0)
    col = lax.broadcasted_iota(jnp.int32, (qb, qb), 1)
    lower = col <= row
    diag = col == row
    dist = (((row - col) & (qb - 1)) * dil).astype(F32)
    if not has_prev:
        dist = jnp.where(lower, dist, jnp.inf)
    elif first is not None:
        dist = jnp.where(jnp.logical_or(lower, jnp.logical_not(first)), dist, jnp.inf)
    far = float(qb * dil)
    lane = lax.broadcasted_iota(jnp.int32, (qb, LANES), 1)
    lane_lo = lane < ATT_HEAD_DIM
    mx_tile = jnp.zeros((qb, LANES), F32)
    den_tile = jnp.ones((qb, LANES), F32)

    for hb in range(0, ATT_HEADS, ATT_HEAD_BATCH):
        heads = list(range(hb, hb + ATT_HEAD_BATCH))
        kc, vc, kp, vp = {}, {}, {}, {}
        for pr in sorted({h // 2 for h in heads}):
            sl = slice(pr * LANES, (pr + 1) * LANES)
            vsl = slice(ATT_WIDTH + pr * LANES, ATT_WIDTH + (pr + 1) * LANES)
            kc[pr] = kvc_ref[:, sl]
            vc[pr] = kvc_ref[:, vsl]
            if has_prev:
                kp[pr] = kvp_ref[:, sl]
                vp[pr] = kvp_ref[:, vsl]
        qh = [jnp.where(lane_lo if h % 2 == 0 else jnp.logical_not(lane_lo),
                        q[:, (h // 2) * LANES:(h // 2 + 1) * LANES], jnp.zeros((qb, LANES), BF16)) for h in heads]
        slopes = [slope_ref[h] for h in heads]
        s_c = [_dot_nt(qh[i], kc[h // 2]) for i, h in enumerate(heads)]
        if has_prev:
            s_p = [_dot_nt(qh[i], kp[h // 2]) for i, h in enumerate(heads)]
            s = [jnp.where(lower, s_c[i], s_p[i]) - slopes[i] * dist for i in range(len(heads))]
            s_d = [jnp.sum(jnp.where(diag, s_p[i], 0.0), axis=-1, keepdims=True) - slopes[i] * far
                   for i in range(len(heads))]
            if first is not None:
                s_d = [jnp.where(first, -jnp.inf, sd) for sd in s_d]
            mx = [jnp.maximum(jnp.max(s[i], axis=-1, keepdims=True), s_d[i]) for i in range(len(heads))]
            p = [jnp.exp(s[i] - mx[i]) for i in range(len(heads))]
            p_d = [jnp.exp(s_d[i] - mx[i]) for i in range(len(heads))]
            den = [jnp.sum(p[i], axis=-1, keepdims=True) + p_d[i] for i in range(len(heads))]
            acc = [_dot(jnp.where(lower, p[i], 0.0).astype(BF16), vc[h // 2])
                   + _dot(jnp.where(lower, jnp.where(diag, p_d[i], 0.0), p[i]).astype(BF16), vp[h // 2])
                   for i, h in enumerate(heads)]
        else:
            s = [s_c[i] - slopes[i] * dist for i in range(len(heads))]
            mx = [jnp.max(s[i], axis=-1, keepdims=True) for i in range(len(heads))]
            p = [jnp.exp(s[i] - mx[i]) for i in range(len(heads))]
            den = [jnp.sum(p[i], axis=-1, keepdims=True) for i in range(len(heads))]
            acc = [_dot(p[i].astype(BF16), vc[h // 2]) for i, h in enumerate(heads)]
        out = [acc[i] * (1.0 / den[i]) for i in range(len(heads))]
        for i, h in enumerate(heads):
            mx_tile = jnp.where(lane == h, mx[i], mx_tile)
            den_tile = jnp.where(lane == h, den[i], den_tile)
        for i in range(0, len(heads), 2):
            pr = heads[i] // 2
            o_ref[:, pr * LANES:(pr + 1) * LANES] = jnp.where(lane_lo, out[i], out[i + 1]).astype(o_ref.dtype)
    lse_ref[...] = mx_tile + jnp.log(den_tile)


def _attn_prompt_group(q, kv, slopes, g):
    win, dil = DIL_GROUPS[g]
    bsz, _, n, _ = q.shape
    assert win // dil == Q_BLOCK and n % Q_BLOCK == 0
    nblk = n // Q_BLOCK
    has_prev = nblk > 1
    units = ATT_UNITS
    if has_prev:
        assert nblk % units == 0
        grid = (bsz, dil, nblk // units)
        blk = lambda width: pl.BlockSpec((1, 1, units * Q_BLOCK, width), lambda b, r, j: (b, r, j, 0))
    else:
        assert dil % units == 0
        grid = (bsz, dil // units, 1)
        blk = lambda width: pl.BlockSpec((1, units, Q_BLOCK, width), lambda b, r, j: (b, r, 0, 0))
    return pl.pallas_call(
        functools.partial(_attn_prompt_kernel, dil=dil, has_prev=has_prev, units=units),
        grid=grid,
        in_specs=[pl.BlockSpec(memory_space=pltpu.SMEM), blk(ATT_WIDTH), blk(KV_ROW)],
        out_specs=[blk(ATT_WIDTH), blk(LANES)],
        out_shape=[jax.ShapeDtypeStruct((bsz, dil, n, ATT_WIDTH), BF16),
                   jax.ShapeDtypeStruct((bsz, dil, n, LANES), F32)],
        scratch_shapes=[pltpu.VMEM((Q_BLOCK, KV_ROW), BF16)] if has_prev else [],
        compiler_params=_cparams(3),
    )(slopes, q, kv)


def _merge_out_kernel(o0_ref, o1_ref, o2_ref, l0_ref, l1_ref, l2_ref, gate_ref, e_ref, w_ref, r_ref,
                      g_ref, b_ref, out_ref, os_ref, ls_ref, lhs_ref, og_ref):
    for g, (o_ref, l_ref) in enumerate(((o0_ref, l0_ref), (o1_ref, l1_ref), (o2_ref, l2_ref))):
        dil = DIL_GROUPS[g][1]
        if dil > 1:
            _interleave(ls_ref.at[g - 1], l_ref, dil)
            _interleave(os_ref.at[g - 1], o_ref, dil)
    l0, l1, l2 = l0_ref[0, 0], ls_ref[0, 0], ls_ref[1, 0]
    top = jnp.maximum(jnp.maximum(l0, l1), l2)
    w0, w1, w2 = jnp.exp(l0 - top), jnp.exp(l1 - top), jnp.exp(l2 - top)
    inv = 1.0 / (w0 + w1 + w2)
    for i, wn in enumerate((w0 * inv, w1 * inv)):
        hi = wn.astype(BF16)
        lhs_ref[i, :, 0:LANES] = hi
        lhs_ref[i, :, LANES:2 * LANES] = (wn - hi.astype(F32)).astype(BF16)
    for c0 in range(0, ATT_WIDTH, MM_COLS):
        sl = slice(c0, c0 + MM_COLS)
        blocks = range(c0 // LANES, (c0 + MM_COLS) // LANES)
        w0e = _dot(lhs_ref[0], e_ref[:, sl])
        w1e = _dot(lhs_ref[1], e_ref[:, sl])
        o1 = jnp.concatenate([os_ref[0, cb] for cb in blocks], axis=1)
        o2 = jnp.concatenate([os_ref[1, cb] for cb in blocks], axis=1)
        o = w0e * o0_ref[0, 0, :, sl].astype(F32) + w1e * o1 + (1.0 - w0e - w1e) * o2
        og_ref[:, sl] = (o * _silu(gate_ref[:, sl])).astype(BF16)
    v = DEEPNORM_ALPHA * r_ref[...] + _dot(og_ref[...], w_ref[...])
    out_ref[...] = _layer_norm(v, g_ref[...], b_ref[...])


def _merge_out(os_, lses, gate, expand16, w, resid, g, b, seq):
    tm = PROJ_ROWS
    tpb = seq // tm
    m = resid.shape[0]
    rowblk = lambda width: pl.BlockSpec((tm, width), lambda i: (i, 0))
    const = lambda shape: pl.BlockSpec(shape, lambda i: (0, 0))
    resblk = lambda dil, width: pl.BlockSpec((1, dil, tm // dil, width), lambda i: (i // tpb, 0, i % tpb, 0))
    return pl.pallas_call(
        _merge_out_kernel,
        grid=(m // tm,),
        in_specs=[resblk(dil, ATT_WIDTH) for _, dil in DIL_GROUPS] + [resblk(dil, LANES) for _, dil in DIL_GROUPS]
                 + [rowblk(ATT_WIDTH), const((2 * LANES, ATT_WIDTH)), const((ATT_WIDTH, D_MODEL)), rowblk(D_MODEL),
                    const((1, D_MODEL)), const((1, D_MODEL))],
        out_specs=rowblk(D_MODEL),
        out_shape=jax.ShapeDtypeStruct((m, D_MODEL), F32),
        scratch_shapes=[pltpu.VMEM((N_DIL - 1, ATT_WIDTH // LANES, tm, LANES), F32),
                        pltpu.VMEM((N_DIL - 1, 1, tm, LANES), F32),
                        pltpu.VMEM((2, tm, 2 * LANES), BF16),
                        pltpu.VMEM((tm, ATT_WIDTH), BF16)],
        compiler_params=_cparams(1),
    )(*os_, *lses, gate, expand16, w, resid, g, b)


def _cache_shift_kernel(c_ref, new_ref, sel_ref, o_ref, comp_ref, *, dil):
    b = pl.program_id(0)
    x = c_ref[...]
    win = x.shape[1]
    if dil > 1:
        comp_ref[...] = _dot(x.astype(BF16), sel_ref[...]).astype(BF16)
    else:
        comp_ref[...] = x.astype(BF16)
    nv = new_ref[...]
    pick = lax.broadcasted_iota(jnp.int32, nv.shape, 1) == b
    newcol = jnp.sum(jnp.where(pick, nv, 0.0), axis=1, keepdims=True)
    last = lax.broadcasted_iota(jnp.int32, x.shape, 1) == win - 1
    o_ref[...] = jnp.where(last, newcol, pltpu.roll(x, win - 1, axis=1))


def _cache_shift(cache, kvt_new, g):
    win, dil = DIL_GROUPS[g]
    nb = cache.shape[0]
    assert cache.shape[1] == win and win // dil == Q_BLOCK
    ct = cache.transpose(0, 2, 3, 4, 1).reshape(nb * KV_ROW, win)
    rows = min(KV_ROW, SHIFT_BLOCK_BYTES // (4 * win))
    nrb = KV_ROW // rows
    sel = (jnp.arange(win)[:, None] == jnp.arange(Q_BLOCK)[None, :] * dil).astype(BF16)
    out, comp = pl.pallas_call(
        functools.partial(_cache_shift_kernel, dil=dil),
        grid=(nb, nrb),
        in_specs=[pl.BlockSpec((rows, win), lambda b, i: (b * nrb + i, 0)),
                  pl.BlockSpec((rows, nb), lambda b, i: (g * nrb + i, 0)),
                  pl.BlockSpec((win, Q_BLOCK), lambda b, i: (0, 0))],
        out_specs=[pl.BlockSpec((rows, win), lambda b, i: (b * nrb + i, 0)),
                   pl.BlockSpec((rows, Q_BLOCK), lambda b, i: (b * nrb + i, 0))],
        out_shape=[jax.ShapeDtypeStruct((nb * KV_ROW, win), cache.dtype),
                   jax.ShapeDtypeStruct((nb * KV_ROW, Q_BLOCK), BF16)],
        compiler_params=_cparams(2),
    )(ct, kvt_new, sel)
    new_cache = out.reshape(nb, 2, ATT_HEADS, ATT_HEAD_DIM, win).transpose(0, 4, 1, 2, 3)
    return new_cache, comp


def _attn_sample_kernel(slope_ref, q0_ref, q1_ref, q2_ref, gate_ref, n0_ref, n1_ref, n2_ref,
                        c0_ref, c1_ref, c2_ref, o_ref):
    nh = ATT_HEADS
    keys = Q_BLOCK
    lane = lax.broadcasted_iota(jnp.int32, (nh, ATT_WIDTH), 1)
    hrow = lax.broadcasted_iota(jnp.int32, (nh, ATT_WIDTH), 0)
    head_mask = (lane // ATT_HEAD_DIM) == hrow
    kidx = lax.broadcasted_iota(jnp.int32, (nh, keys), 1)
    slope = slope_ref[:, 0:1]
    outs, lses = [], []
    for g, (q_ref, n_ref, c_ref) in enumerate(((q0_ref, n0_ref, c0_ref), (q1_ref, n1_ref, c1_ref),
                                               (q2_ref, n2_ref, c2_ref))):
        dil = DIL_GROUPS[g][1]
        q = q_ref[0] * ATT_SCALE
        qm = jnp.where(head_mask, jnp.broadcast_to(q, (nh, ATT_WIDTH)), 0.0)
        new = n_ref[0]
        k_t = c_ref[0:ATT_WIDTH, :]
        v_t = c_ref[ATT_WIDTH:KV_ROW, :]
        dist = ((keys - kidx) * dil).astype(F32)
        s = _dot(qm.astype(BF16), k_t) - slope * dist
        s_new = jnp.sum(qm * new[:, 0:ATT_WIDTH], axis=-1, keepdims=True)
        mx = jnp.maximum(jnp.max(s, axis=-1, keepdims=True), s_new)
        p = jnp.exp(s - mx)
        p_new = jnp.exp(s_new - mx)
        den = jnp.sum(p, axis=-1, keepdims=True) + p_new
        outs.append((_dot_nt(p.astype(BF16), v_t) + p_new * new[:, ATT_WIDTH:]) / den)
        lses.append(mx + jnp.log(den))
    top = jnp.maximum(jnp.maximum(lses[0], lses[1]), lses[2])
    ws = [jnp.exp(l - top) for l in lses]
    o = (ws[0] * outs[0] + ws[1] * outs[1] + ws[2] * outs[2]) / (ws[0] + ws[1] + ws[2])
    o = jnp.sum(jnp.where(head_mask, o, 0.0), axis=0, keepdims=True)
    o_ref[0] = o * _silu(gate_ref[0])


def _attn_sample(proj, kv_new, comps, slopes_b):
    nb = proj.shape[0]
    proj3 = proj.reshape(nb, 1, 4 * ATT_WIDTH)
    kvn3 = kv_new.reshape(nb, 1, N_DIL * KV_ROW)
    qspec = lambda g: pl.BlockSpec((1, 1, ATT_WIDTH), lambda b: (b, 0, g))
    nspec = lambda g: pl.BlockSpec((1, 1, KV_ROW), lambda b: (b, 0, g))
    cspec = pl.BlockSpec((KV_ROW, Q_BLOCK), lambda b: (b, 0))
    o = pl.pallas_call(
        _attn_sample_kernel,
        grid=(nb,),
        in_specs=[pl.BlockSpec((ATT_HEADS, LANES), lambda b: (0, 0)),
                  qspec(0), qspec(1), qspec(2), qspec(3), nspec(0), nspec(1), nspec(2),
                  cspec, cspec, cspec],
        out_specs=pl.BlockSpec((1, 1, ATT_WIDTH), lambda b: (b, 0, 0)),
        out_shape=jax.ShapeDtypeStruct((nb, 1, ATT_WIDTH), F32),
        compiler_params=_cparams(1),
    )(slopes_b, proj3, proj3, proj3, proj3, kvn3, kvn3, kvn3, *comps)
    return o.reshape(nb, ATT_WIDTH)


def _pad_lanes(v):
    return jnp.pad(v.astype(F32), (0, LANES - v.shape[0])).reshape(1, LANES)


def kernel(x_prompt, x_sample, state_ssm, state_conv, cache_kv_w128, cache_kv_w512, cache_kv_w2048,
           a_in_proj, a_conv_w, a_conv_b, a_dt_bias, a_log, a_d, a_norm_w, a_out_proj,
           kv_proj, b_in_proj, b_out_proj, ln_g, ln_b):
    bsz, seq, _ = x_prompt.shape
    nb = x_sample.shape[0]
    assert x_sample.shape[1] == 1 and seq % CHUNK == 0
    caches = (cache_kv_w128, cache_kv_w512, cache_kv_w2048)

    heads = jnp.arange(LANES)[:, None]
    expand32 = (heads == jnp.arange(D_INNER)[None, :] // SSM_HEAD_DIM).astype(BF16)
    expand16 = (heads == jnp.arange(ATT_WIDTH)[None, :] // ATT_HEAD_DIM).astype(BF16)
    expand32 = jnp.concatenate([expand32, expand32], axis=0)
    expand16 = jnp.concatenate([expand16, expand16], axis=0)
    tril = (jnp.arange(CHUNK)[:, None] >= jnp.arange(CHUNK)[None, :]).astype(BF16)
    eye = jnp.eye(LANES, dtype=BF16)
    p = jnp.arange(CONV_WIDTH * CHUNK)
    blk = CONV_WIDTH * SUBLANES
    tap_i = (p // blk) * SUBLANES + p % SUBLANES
    tap_k = (p % blk) // SUBLANES
    shift = (jnp.arange(2 * CHUNK)[None, :] == (CHUNK + tap_i - (CONV_WIDTH - 1 - tap_k))[:, None]).astype(BF16)
    slopes =jnp.exp2(-8.0 * jnp.arange(1, ATT_HEADS + 1, dtype=F32) / ATT_HEADS)
    slopes_b = jnp.broadcast_to(slopes[:, None], (ATT_HEADS, LANES))

    hp = x_prompt.reshape(bsz * seq, D_MODEL)
    hs = x_sample.reshape(nb, D_MODEL)
    stacks_p, stacks_s = None, None

    for i in range(N_A_LAYERS):
        w_in = a_in_proj[i]
        w_main = w_in[:, 0:D_INNER + CONV_DIM].astype(BF16)
        w_dt = jnp.pad(w_in[:, D_INNER + CONV_DIM:], ((0, 0), (0, LANES - SSM_HEADS))).astype(BF16)
        w_out = a_out_proj[i].astype(BF16)
        prm = dict(
            cwx=a_conv_w[i][:, 0:D_INNER], cwbc=a_conv_w[i][:, D_INNER:],
            cbx=a_conv_b[i][0:D_INNER].reshape(1, D_INNER), cbbc=a_conv_b[i][D_INNER:].reshape(1, BC_WIDTH),
            dtb=_pad_lanes(a_dt_bias[i]), alog=_pad_lanes(a_log[i]),
            dexp=jnp.repeat(a_d[i].astype(F32), SSM_HEAD_DIM).reshape(1, D_INNER),
            nw=a_norm_w[i].reshape(1, D_INNER), expand2=expand32, tril=tril, eye=eye, shift=shift)
        g_ln, b_ln = ln_g[i].reshape(1, D_MODEL), ln_b[i].reshape(1, D_MODEL)

        zx, dtp = _a_in_proj(hp, w_main, w_dt, BF16)
        yn, *stacks_p = _ssd_prompt(zx, dtp, prm, bsz, seq, i, stacks_p)
        hp = _matmul_ln(yn, w_out, hp, g_ln, b_ln)

        zx_s, dt_s = _a_in_proj(hs, w_main, w_dt, F32)
        yn_s, *stacks_s = _ssd_sample(zx_s, dt_s, state_conv, state_ssm, prm, i, stacks_s)
        hs = _matmul_ln(yn_s, w_out, hs, g_ln, b_ln)
    ssm_p, conv_p = stacks_p
    conv_s, ssm_s = stacks_s

    kvw = kv_proj.reshape(D_MODEL, 2, N_DIL, ATT_WIDTH)
    kvw_g = [jnp.concatenate([kvw[:, 0, g], kvw[:, 1, g]], axis=1).astype(BF16) for g in range(N_DIL)]
    kvw_all = jnp.concatenate(kvw_g, axis=1)
    kv_p, new_kv_p = [], []
    for g in range(N_DIL):
        kv_res, kv_t = _kv_proj(hp, kvw_g[g], g, bsz, seq)
        kv_p.append(kv_res)
        new_kv_p.append(kv_t.reshape(bsz, 2, ATT_HEADS, ATT_HEAD_DIM, kv_t.shape[-1]).transpose(0, 4, 1, 2, 3))
    kv_s = _matmul(hs, kvw_all, tn=1024)
    kvt_s = _matmul_nt(kvw_all.T, hs, tn=1024)
    new_kv_s, comps = [], []
    for g in range(N_DIL):
        shifted, comp = _cache_shift(caches[g], kvt_s, g)
        new_kv_s.append(shifted)
        comps.append(comp)

    for j in range(N_B_LAYERS):
        layer = N_A_LAYERS + j
        w_in = b_in_proj[j].astype(BF16)
        w_out = b_out_proj[j].astype(BF16)
        g_ln, b_ln = ln_g[layer].reshape(1, D_MODEL), ln_b[layer].reshape(1, D_MODEL)

        q0, q1, q2, gate = _b_in_proj(hp, w_in, bsz, seq)
        res = [_attn_prompt_group(q, kv_p[g], slopes, g) for g, q in enumerate((q0, q1, q2))]
        hp = _merge_out([r[0] for r in res], [r[1] for r in res], gate, expand16, w_out, hp, g_ln, b_ln, seq)

        proj_s = _matmul(hs, w_in, tn=1024)
        og_s = _attn_sample(proj_s, kv_s, comps, slopes_b)
        hs = _matmul_ln(og_s, w_out, hs, g_ln, b_ln)

    return (hp.reshape(bsz, seq, D_MODEL), hs.reshape(nb, 1, D_MODEL),
            ssm_p, conv_p, new_kv_p[0], new_kv_p[1], new_kv_p[2],
            ssm_s, conv_s, new_kv_s[0], new_kv_s[1], new_kv_s[2])
```

```python
import functools

import jax
import jax.numpy as jnp
from jax import lax
from jax.experimental import pallas as pl
from jax.experimental.pallas import tpu as pltpu

F32 = jnp.float32
BF16 = jnp.bfloat16

D_MODEL = 1024
N_A_LAYERS = 2
N_B_LAYERS = 2
D_INNER = 2048
SSM_HEAD_DIM = 64
SSM_HEADS = 32
SSM_GROUPS = 4
SSM_STATE = 128
HEADS_PER_GROUP = SSM_HEADS // SSM_GROUPS
GROUP_WIDTH = D_INNER // SSM_GROUPS
CONV_WIDTH = 4
BC_WIDTH = 2 * SSM_GROUPS * SSM_STATE
CONV_DIM = D_INNER + BC_WIDTH
CHUNK = 128
DIL_GROUPS = ((128, 1), (512, 4), (2048, 16))
N_DIL = 3
ATT_HEADS = 16
ATT_HEAD_DIM = 64
ATT_WIDTH = 1024
KV_ROW = 2 * ATT_WIDTH
Q_BLOCK = 128
ATT_SCALE = ATT_HEAD_DIM ** -0.5
LN_EPS = 1e-5
RMS_EPS = 1e-5
DEEPNORM_ALPHA = (2.0 * 4) ** 0.25

LANES = 128
SUBLANES = 8
VMEM_LIMIT = 48 * 1024 * 1024

ATT_UNITS = 2
ATT_HEAD_BATCH = 16
PROJ_ROWS = 512
PROJ_COLS = 512
MM_COLS = 256
KV_ROWS = 512


def _cparams(n_grid):
    return pltpu.CompilerParams(dimension_semantics=("arbitrary",) * n_grid, vmem_limit_bytes=VMEM_LIMIT)


def _silu(x):
    h = 0.5 * x
    return h + h * jnp.tanh(h)


def _softplus(x):
    return jnp.maximum(x, 0.0) + jnp.log1p(jnp.exp(-jnp.abs(x)))


def _split3(a):
    hi = a.astype(BF16)
    r1 = a - hi.astype(F32)
    mid = r1.astype(BF16)
    lo = (r1 - mid.astype(F32)).astype(BF16)
    return hi, mid, lo


def _dot(a, b):
    return jnp.dot(a, b, preferred_element_type=F32)


def _dot_nt(a, b):
    return lax.dot_general(a, b, (((1,), (1,)), ((), ())), preferred_element_type=F32)


def _expand_dot(a, m01_twice):
    hi = a.astype(BF16)
    lo = (a - hi.astype(F32)).astype(BF16)
    return _dot(jnp.concatenate([hi, lo], axis=1), m01_twice)


def _exact_dot_left(m01, a):
    hi, mid, lo = _split3(a)
    return _dot(m01, hi) + _dot(m01, mid) + _dot(m01, lo)


def _layer_norm(v, g, b):
    mu = jnp.mean(v, axis=-1, keepdims=True)
    d = v - mu
    var = jnp.mean(d * d, axis=-1, keepdims=True)
    return d * lax.rsqrt(var + LN_EPS) * g + b


def _lane_blocks(width):
    return [slice(cb * LANES, (cb + 1) * LANES) for cb in range(width // LANES)]


def _deinterleave(dst_ref, blk_ref, dil):
    rows = blk_ref.shape[1] // dil
    for r in range(dil):
        for cb, sl in enumerate(_lane_blocks(dst_ref.shape[-1])):
            dst_ref[0, r, :, sl] = blk_ref[cb, pl.ds(r, rows, stride=dil), :].astype(dst_ref.dtype)


def _interleave(blk_ref, src_ref, dil):
    rows = blk_ref.shape[1] // dil
    for r in range(dil):
        for cb, sl in enumerate(_lane_blocks(src_ref.shape[-1])):
            blk_ref[cb, pl.ds(r, rows, stride=dil), :] = src_ref[0, r, :, sl].astype(blk_ref.dtype)


def _mm_kernel(x_ref, w_ref, o_ref, xb_ref):
    @pl.when(pl.program_id(1) == 0)
    def _():
        xb_ref[...] = x_ref[...].astype(BF16)

    o_ref[...] = _dot(xb_ref[...], w_ref[...]).astype(o_ref.dtype)


def _matmul(x, w, *, tn, out_dtype=F32, tm=1024):
    m, k = x.shape
    n = w.shape[1]
    tm = min(tm, m)
    assert m % tm == 0 and n % tn == 0
    return pl.pallas_call(
        _mm_kernel,
        grid=(m // tm, n // tn),
        in_specs=[pl.BlockSpec((tm, k), lambda i, j: (i, 0)),
                  pl.BlockSpec((k, tn), lambda i, j: (0, j))],
        out_specs=pl.BlockSpec((tm, tn), lambda i, j: (i, j)),
        out_shape=jax.ShapeDtypeStruct((m, n), out_dtype),
        scratch_shapes=[pltpu.VMEM((tm, k), BF16)],
        compiler_params=_cparams(2),
    )(x, w)


def _a_in_proj_kernel(x_ref, w_ref, wdt_ref, zx_ref, dt_ref):
    xb = x_ref[...].astype(BF16)
    for c in range(w_ref.shape[1] // PROJ_COLS):
        sl = slice(c * PROJ_COLS, (c + 1) * PROJ_COLS)
        zx_ref[:, sl] = _dot(xb, w_ref[:, sl]).astype(zx_ref.dtype)
    dt_ref[...] = _dot(xb, wdt_ref[...])


def _a_in_proj(x, w_main, w_dt, out_dtype):
    m, k = x.shape
    n = w_main.shape[1]
    tm = min(PROJ_ROWS, m)
    assert m % tm == 0 and n % PROJ_COLS == 0
    return pl.pallas_call(
        _a_in_proj_kernel,
        grid=(m // tm,),
        in_specs=[pl.BlockSpec((tm, k), lambda i: (i, 0)),
                  pl.BlockSpec((k, n), lambda i: (0, 0)),
                  pl.BlockSpec((k, LANES), lambda i: (0, 0))],
        out_specs=[pl.BlockSpec((tm, n), lambda i: (i, 0)),
                   pl.BlockSpec((tm, LANES), lambda i: (i, 0))],
        out_shape=[jax.ShapeDtypeStruct((m, n), out_dtype),
                   jax.ShapeDtypeStruct((m, LANES), F32)],
        compiler_params=_cparams(1),
    )(x, w_main, w_dt)


def _mm_nt_kernel(w_ref, x_ref, o_ref):
    o_ref[...] = _dot_nt(w_ref[...], x_ref[...].astype(BF16))


def _matmul_nt(w_t, x, *, tn):
    n, k = w_t.shape
    m = x.shape[0]
    assert n % tn == 0
    return pl.pallas_call(
        _mm_nt_kernel,
        grid=(n // tn,),
        in_specs=[pl.BlockSpec((tn, k), lambda i: (i, 0)),
                  pl.BlockSpec((m, k), lambda i: (0, 0))],
        out_specs=pl.BlockSpec((tn, m), lambda i: (i, 0)),
        out_shape=jax.ShapeDtypeStruct((n, m), F32),
        compiler_params=_cparams(1),
    )(w_t, x)


def _mm_ln_kernel(y_ref, w_ref, r_ref, g_ref, b_ref, o_ref):
    acc = _dot(y_ref[...].astype(BF16), w_ref[...])
    v = DEEPNORM_ALPHA * r_ref[...] + acc
    o_ref[...] = _layer_norm(v, g_ref[...], b_ref[...])


def _matmul_ln(y, w, resid, g, b, *, tm=512):
    m, k = y.shape
    n = w.shape[1]
    tm = min(tm, m)
    assert m % tm == 0
    return pl.pallas_call(
        _mm_ln_kernel,
        grid=(m // tm,),
        in_specs=[pl.BlockSpec((tm, k), lambda i: (i, 0)),
                  pl.BlockSpec((k, n), lambda i: (0, 0)),
                  pl.BlockSpec((tm, n), lambda i: (i, 0)),
                  pl.BlockSpec((1, n), lambda i: (0, 0)),
                  pl.BlockSpec((1, n), lambda i: (0, 0))],
        out_specs=pl.BlockSpec((tm, n), lambda i: (i, 0)),
        out_shape=jax.ShapeDtypeStruct((m, n), F32),
        compiler_params=_cparams(1),
    )(y, w, resid, g, b)


def _ssd_prompt_kernel(z_ref, xr_ref, bcr_ref, dtr_ref, cwx_ref, cwbc_ref, cbx_ref, cbbc_ref,
                       dtb_ref, alog_ref, dexp_ref, nw_ref, e_ref, tril_ref, shift_ref,
                       y_ref, ssm_ref, conv_ref,
                       st_ref, extx_ref, extbc_ref, lhs_ref, bc_ref, xs_ref, xdtb_ref, xwb_ref, eae_ref,
                       cde_ref, yp_ref, *, ride):
    c = pl.program_id(1)
    t = CHUNK
    wide = 2 * LANES
    conv_ref = conv_ref.at[0, 0]

    @pl.when(c == 0)
    def _():
        st_ref[...] = jnp.zeros_like(st_ref)
        extx_ref[0:t, :] = jnp.zeros((t, D_INNER), BF16)
        extbc_ref[0:t, :] = jnp.zeros((t, BC_WIDTH), BF16)

    ride()

    dt = _softplus(dtr_ref[...] + dtb_ref[...])
    a = -jnp.exp(alog_ref[...])
    acs = _exact_dot_left(tril_ref[...], dt * a)
    acs_t = acs.T
    a_last = acs[t - 1:t, :]
    stacked = jnp.concatenate(
        [dt, jnp.exp(acs), jnp.exp(a_last - acs), jnp.broadcast_to(jnp.exp(a_last), (2 * SUBLANES, LANES))], axis=0)
    hi = stacked.astype(BF16)
    lhs_ref[:, 0:LANES] = hi
    lhs_ref[:, LANES:wide] = (stacked - hi.astype(F32)).astype(BF16)

    extx_ref[t:2 * t, :] = xr_ref[...]
    extbc_ref[t:2 * t, :] = bcr_ref[...]

    def conv(ext_ref, w_ref, b_ref, sl):
        taps = _dot(shift_ref[...], ext_ref[:, sl]).reshape(t // SUBLANES, CONV_WIDTH, SUBLANES, wide)
        acc = jnp.broadcast_to(b_ref[:, sl], (SUBLANES, wide))
        for k in range(CONV_WIDTH):
            acc = acc + taps[:, k] * w_ref[k:k + 1, sl]
        tail = taps[t // SUBLANES - 1, CONV_WIDTH - 1, SUBLANES - (CONV_WIDTH - 1):SUBLANES]
        return acc.reshape(t, wide), tail

    for cb in range(BC_WIDTH // wide):
        sl = slice(cb * wide, (cb + 1) * wide)
        acc, tail = conv(extbc_ref, cwbc_ref, cbbc_ref, sl)
        conv_ref[:, D_INNER + cb * wide:D_INNER + (cb + 1) * wide] = tail
        bc_ref[:, sl] = _silu(acc)
    for cb in range(D_INNER // wide):
        sl = slice(cb * wide, (cb + 1) * wide)
        acc, tail = conv(extx_ref, cwx_ref, cbx_ref, sl)
        conv_ref[:, sl] = tail
        xs = _silu(acc)
        ex = _dot(lhs_ref[...], e_ref[:, sl])
        xdt = xs * ex[0:t]
        xs_ref[:, sl] = xs
        xdtb_ref[:, sl] = xdt.astype(BF16)
        xwb_ref[:, sl] = (xdt * ex[2 * t:3 * t]).astype(BF16)
        eae_ref[:, sl] = ex[t:2 * t]
        cde_ref[:, sl] = ex[3 * t:3 * t + SUBLANES]
    extx_ref[0:t, :] = xr_ref[...]
    extbc_ref[0:t, :] = bcr_ref[...]

    row = lax.broadcasted_iota(jnp.int32, (t, t), 0)
    col = lax.broadcasted_iota(jnp.int32, (t, t), 1)
    causal = row >= col
    lane_lo = lax.broadcasted_iota(jnp.int32, (t, LANES), 1) < SSM_HEAD_DIM

    for g in range(SSM_GROUPS):
        g0 = g * GROUP_WIDTH
        gsl = slice(g0, g0 + GROUP_WIDTH)
        bg = bc_ref[:, g * SSM_STATE:(g + 1) * SSM_STATE]
        cg_b = bc_ref[:, (SSM_GROUPS + g) * SSM_STATE:(SSM_GROUPS + g + 1) * SSM_STATE].astype(BF16)
        cb = _dot_nt(cg_b, bg.astype(BF16))
        s_old = st_ref[:, gsl]
        yp_ref[:, gsl] = _dot(cg_b, s_old.astype(BF16)) * eae_ref[:, gsl]
        states = _dot(bg.T.astype(BF16), xwb_ref[:, gsl])
        st_ref[:, gsl] = s_old * cde_ref[0:1, gsl] + states
        sq = jnp.zeros((t, LANES), F32)
        for pr in range(HEADS_PER_GROUP // 2):
            sl = slice(g0 + pr * LANES, g0 + (pr + 1) * LANES)
            xp = xdtb_ref[:, sl]
            halves = []
            for half in range(2):
                h = g * HEADS_PER_GROUP + 2 * pr + half
                seg = acs[:, h:h + 1] - acs_t[h:h + 1, :]
                dec = jnp.exp(jnp.where(causal, seg, -jnp.inf))
                halves.append(_dot((cb * dec).astype(BF16), xp))
            y = jnp.where(lane_lo, halves[0], halves[1]) + yp_ref[:, sl] + xs_ref[:, sl] * dexp_ref[:, sl]
            hz = y * _silu(z_ref[:, sl].astype(F32))
            yp_ref[:, sl] = hz
            sq = sq + hz * hz
        scale = lax.rsqrt(jnp.sum(sq, axis=-1, keepdims=True) * (1.0 / GROUP_WIDTH) + RMS_EPS)
        y_ref[:, gsl] = (yp_ref[:, gsl] * scale * nw_ref[:, gsl]).astype(y_ref.dtype)

    @pl.when(c == pl.num_programs(1) - 1)
    def _():
        for j in range(D_INNER // LANES):
            tile = st_ref[:, j * LANES:(j + 1) * LANES].T
            ssm_ref[0, 0, 2 * j:2 * j + 2] = tile.reshape(2, SSM_HEAD_DIM, SSM_STATE)


def _skip_carried(body, n_in, n_carry):
    def kern(*refs):
        body(*refs[:n_in], *refs[n_in + n_carry:])
    return kern


def _ssd_prompt(zx, dtp, prm, bsz, seq, layer, carried, riders):
    nc = seq // CHUNK
    m = bsz * seq
    row = lambda b, c: b * nc + c
    const = lambda shape: pl.BlockSpec(shape, lambda b, c: (0,) * len(shape))
    ins = [zx, zx, zx, dtp, prm["cwx"], prm["cwbc"], prm["cbx"], prm["cbbc"], prm["dtb"], prm["alog"],
           prm["dexp"], prm["nw"], prm["expand2"], prm["tril"], prm["shift"]]
    extra = list(carried)
    n_in, n_ride = len(ins), 2 * len(riders)
    aliases = {n_in + n_ride: 1, n_in + n_ride + 1: 2}
    ride_in_specs, ride_out_specs, ride_ins, ride_out_shapes = [], [], [], []
    for rd in riders:
        blk = lambda width, rd=rd: pl.BlockSpec((rd["rows"], width), lambda b, c: (row(b, c), 0))
        ride_ins += rd["ins"]
        ride_in_specs += [blk(rd["win"]), const((rd["win"], Q_BLOCK))]
        ride_out_specs += [blk(rd["win"]), blk(Q_BLOCK)]
        ride_out_shapes += rd["out_shape"]

    def kern(*refs):
        out0 = n_in + n_ride + len(extra)
        ride_in, ride_out = refs[n_in:n_in + n_ride], refs[out0 + 3:out0 + 3 + n_ride]

        def ride():
            for i, rd in enumerate(riders):
                _cache_shift_block(ride_in[2 * i], ride_in[2 * i + 1], ride_out[2 * i], ride_out[2 * i + 1], rd["dil"])

        _ssd_prompt_kernel(*refs[:n_in], *refs[out0:out0 + 3], *refs[out0 + 3 + n_ride:], ride=ride)

    return pl.pallas_call(
        kern,
        grid=(bsz, nc),
        in_specs=[pl.BlockSpec((CHUNK, D_INNER), lambda b, c: (row(b, c), 0)),
                  pl.BlockSpec((CHUNK, D_INNER), lambda b, c: (row(b, c), 1)),
                  pl.BlockSpec((CHUNK, BC_WIDTH), lambda b, c: (row(b, c), 4)),
                  pl.BlockSpec((CHUNK, LANES), lambda b, c: (row(b, c), 0)),
                  const((CONV_WIDTH, D_INNER)), const((CONV_WIDTH, BC_WIDTH)),
                  const((1, D_INNER)), const((1, BC_WIDTH)),
                  const((1, LANES)), const((1, LANES)),
                  const((1, D_INNER)), const((1, D_INNER)),
                  const((2 * LANES, D_INNER)), const((CHUNK, CHUNK)), const((CONV_WIDTH * CHUNK, 2 * CHUNK))]
                 + ride_in_specs + [pl.BlockSpec(memory_space=pl.ANY)] * len(extra),
        out_specs=[pl.BlockSpec((CHUNK, D_INNER), lambda b, c: (row(b, c), 0)),
                   pl.BlockSpec((1, 1, SSM_HEADS, SSM_HEAD_DIM, SSM_STATE), lambda b, c: (layer, b, 0, 0, 0)),
                   pl.BlockSpec((1, 1, CONV_WIDTH - 1, CONV_DIM), lambda b, c: (layer, b, 0, 0))]
                  + ride_out_specs,
        out_shape=[jax.ShapeDtypeStruct((m, D_INNER), BF16),
                   jax.ShapeDtypeStruct((N_A_LAYERS, bsz, SSM_HEADS, SSM_HEAD_DIM, SSM_STATE), F32),
                   jax.ShapeDtypeStruct((N_A_LAYERS, bsz, CONV_WIDTH - 1, CONV_DIM), F32)]
                  + ride_out_shapes,
        scratch_shapes=[pltpu.VMEM((SSM_STATE, D_INNER), F32),
                        pltpu.VMEM((2 * CHUNK, D_INNER), BF16),
                        pltpu.VMEM((2 * CHUNK, BC_WIDTH), BF16),
                        pltpu.VMEM((3 * CHUNK + 2 * SUBLANES, 2 * LANES), BF16),
                        pltpu.VMEM((CHUNK, BC_WIDTH), F32),
                        pltpu.VMEM((CHUNK, D_INNER), F32),
                        pltpu.VMEM((CHUNK, D_INNER), BF16),
                        pltpu.VMEM((CHUNK, D_INNER), BF16),
                        pltpu.VMEM((CHUNK, D_INNER), F32),
                        pltpu.VMEM((SUBLANES, D_INNER), F32),
                        pltpu.VMEM((CHUNK, D_INNER), F32)],
        input_output_aliases=aliases,
        compiler_params=_cparams(2),
    )(*ins, *ride_ins, *extra)


def _ssd_sample_kernel(z_ref, xr_ref, bcr_ref, dtr_ref, conv_ref, ssm_ref,
                       cwx_ref, cwbc_ref, cbx_ref, cbbc_ref, dtb_ref, alog_ref, dexp_ref, nw_ref,
                       e_ref, eye_ref,
                       y_ref, convo_ref, ssmo_ref, st_ref):
    prev = conv_ref[0, 0]
    raw = jnp.concatenate([xr_ref[0], bcr_ref[0]], axis=1)
    cw = jnp.concatenate([cwx_ref[...], cwbc_ref[...]], axis=1)
    cbias = jnp.concatenate([cbx_ref[...], cbbc_ref[...]], axis=1)
    acc = cbias + raw * cw[CONV_WIDTH - 1:CONV_WIDTH, :]
    for k in range(CONV_WIDTH - 1):
        acc = acc + prev[k:k + 1, :] * cw[k:k + 1, :]
    convo_ref[0, 0, 0:CONV_WIDTH - 2, :] = prev[1:CONV_WIDTH - 1, :]
    convo_ref[0, 0, CONV_WIDTH - 2:CONV_WIDTH - 1, :] = raw
    xbc = _silu(acc)
    xs = xbc[:, 0:D_INNER]
    bm = xbc[:, D_INNER:D_INNER + SSM_GROUPS * SSM_STATE]
    cm = xbc[:, D_INNER + SSM_GROUPS * SSM_STATE:]

    dt = _softplus(dtr_ref[0] + dtb_ref[...])
    dec = jnp.exp(dt * -jnp.exp(alog_ref[...]))
    stacked = jnp.concatenate([dt, dec, jnp.zeros((SUBLANES - 2, LANES), F32)], axis=0)
    ex = _expand_dot(stacked, e_ref[...])
    dt_e = ex[0:1]
    dec_e = ex[1:2]
    xdt = xs * dt_e

    rows = jnp.concatenate([bm[:, g * SSM_STATE:(g + 1) * SSM_STATE] for g in range(SSM_GROUPS)]
                           + [cm[:, g * SSM_STATE:(g + 1) * SSM_STATE] for g in range(SSM_GROUPS)], axis=0)
    hi, mid, lo = _split3(rows)
    eye = eye_ref[...]
    cols = _dot_nt(eye, hi) + _dot_nt(eye, mid) + _dot_nt(eye, lo)

    for j in range(D_INNER // LANES):
        tile = ssm_ref[0, 0, 2 * j:2 * j + 2].reshape(LANES, SSM_STATE)
        st_ref[:, j * LANES:(j + 1) * LANES] = tile.T
    y_groups = []
    for g in range(SSM_GROUPS):
        g0 = g * GROUP_WIDTH
        new = (st_ref[:, g0:g0 + GROUP_WIDTH] * dec_e[:, g0:g0 + GROUP_WIDTH]
               + cols[:, g:g + 1] * xdt[:, g0:g0 + GROUP_WIDTH])
        st_ref[:, g0:g0 + GROUP_WIDTH] = new
        y_groups.append(jnp.sum(new * cols[:, SSM_GROUPS + g:SSM_GROUPS + g + 1], axis=0, keepdims=True))
    for j in range(D_INNER // LANES):
        tile = st_ref[:, j * LANES:(j + 1) * LANES].T
        ssmo_ref[0, 0, 2 * j:2 * j + 2] = tile.reshape(2, SSM_HEAD_DIM, SSM_STATE)
    y = jnp.concatenate(y_groups, axis=1) + xs * dexp_ref[...]

    hz = y * _silu(z_ref[0])
    normed = []
    for g in range(SSM_GROUPS):
        hg = hz[:, g * GROUP_WIDTH:(g + 1) * GROUP_WIDTH]
        normed.append(hg * lax.rsqrt(jnp.mean(hg * hg, axis=-1, keepdims=True) + RMS_EPS))
    y_ref[0] = jnp.concatenate(normed, axis=1) * nw_ref[...]


def _ssd_sample(zx, dts, conv_state, ssm_state, prm, layer, carried):
    nb = zx.shape[0]
    zx3 = zx.reshape(nb, 1, zx.shape[1])
    dt3 = dts.reshape(nb, 1, LANES)
    const = lambda shape: pl.BlockSpec(shape, lambda b: (0,) * len(shape))
    conv_blk = pl.BlockSpec((1, 1, CONV_WIDTH - 1, CONV_DIM), lambda b: (layer, b, 0, 0))
    ssm_blk = pl.BlockSpec((1, 1, SSM_HEADS, SSM_HEAD_DIM, SSM_STATE), lambda b: (layer, b, 0, 0, 0))
    ins = [zx3, zx3, zx3, dt3, conv_state, ssm_state, prm["cwx"], prm["cwbc"], prm["cbx"], prm["cbbc"],
           prm["dtb"], prm["alog"], prm["dexp"], prm["nw"], prm["expand2"], prm["eye"]]
    extra = list(carried)
    aliases = {len(ins): 1, len(ins) + 1: 2}
    y, conv_new, ssm_new = pl.pallas_call(
        _skip_carried(_ssd_sample_kernel, len(ins), len(extra)),
        grid=(nb,),
        in_specs=[pl.BlockSpec((1, 1, D_INNER), lambda b: (b, 0, 0)),
                  pl.BlockSpec((1, 1, D_INNER), lambda b: (b, 0, 1)),
                  pl.BlockSpec((1, 1, BC_WIDTH), lambda b: (b, 0, 4)),
                  pl.BlockSpec((1, 1, LANES), lambda b: (b, 0, 0)),
                  conv_blk, ssm_blk,
                  const((CONV_WIDTH, D_INNER)), const((CONV_WIDTH, BC_WIDTH)),
                  const((1, D_INNER)), const((1, BC_WIDTH)),
                  const((1, LANES)), const((1, LANES)),
                  const((1, D_INNER)), const((1, D_INNER)),
                  const((2 * LANES, D_INNER)), const((LANES, LANES))]
                 + [pl.BlockSpec(memory_space=pl.ANY)] * len(extra),
        out_specs=[pl.BlockSpec((1, 1, D_INNER), lambda b: (b, 0, 0)), conv_blk, ssm_blk],
        out_shape=[jax.ShapeDtypeStruct((nb, 1, D_INNER), F32),
                   jax.ShapeDtypeStruct(conv_state.shape, F32),
                   jax.ShapeDtypeStruct(ssm_state.shape, F32)],
        scratch_shapes=[pltpu.VMEM((SSM_STATE, D_INNER), F32)],
        input_output_aliases=aliases,
        compiler_params=_cparams(1),
    )(*ins, *extra)
    return y.reshape(nb, D_INNER), conv_new, ssm_new


def _b_in_proj_kernel(x_ref, w_ref, q0_ref, q1_ref, q2_ref, gate_ref, acc_ref):
    xb = x_ref[...].astype(BF16)
    for g, dst in enumerate((q0_ref, q1_ref, q2_ref)):
        dil = DIL_GROUPS[g][1]
        for c0 in range(0, ATT_WIDTH, MM_COLS):
            acc = _dot(xb, w_ref[:, g * ATT_WIDTH + c0:g * ATT_WIDTH + c0 + MM_COLS])
            if dil == 1:
                dst[0, 0, :, c0:c0 + MM_COLS] = acc.astype(dst.dtype)
            else:
                for cb in range(MM_COLS // LANES):
                    acc_ref[c0 // LANES + cb] = acc[:, cb * LANES:(cb + 1) * LANES]
        if dil > 1:
            _deinterleave(dst, acc_ref, dil)
    for c0 in range(0, ATT_WIDTH, PROJ_COLS):
        gate_ref[:, c0:c0 + PROJ_COLS] = _dot(xb, w_ref[:, N_DIL * ATT_WIDTH + c0:N_DIL * ATT_WIDTH + c0 + PROJ_COLS])


def _b_in_proj(x, w, bsz, seq):
    tm = PROJ_ROWS
    tpb = seq // tm
    m, k = x.shape
    qspecs, qshapes = [], []
    for _, dil in DIL_GROUPS:
        assert tm % (dil * 2 * SUBLANES) == 0
        qspecs.append(pl.BlockSpec((1, dil, tm // dil, ATT_WIDTH), lambda i: (i // tpb, 0, i % tpb, 0)))
        qshapes.append(jax.ShapeDtypeStruct((bsz, dil, seq // dil, ATT_WIDTH), BF16))
    return pl.pallas_call(
        _b_in_proj_kernel,
        grid=(m // tm,),
        in_specs=[pl.BlockSpec((tm, k), lambda i: (i, 0)),
                  pl.BlockSpec(w.shape, lambda i: (0, 0))],
        out_specs=qspecs + [pl.BlockSpec((tm, ATT_WIDTH), lambda i: (i, 0))],
        out_shape=qshapes + [jax.ShapeDtypeStruct((m, ATT_WIDTH), F32)],
        scratch_shapes=[pltpu.VMEM((ATT_WIDTH // LANES, tm, LANES), F32)],
        compiler_params=_cparams(1),
    )(x, w)


def _kv_proj_kernel(x_ref, w_ref, kv_ref, kvt_ref, acc_ref, *, dil, first_tile, wt):
    t = pl.program_id(1)
    xb = x_ref[...].astype(BF16)
    for c0 in range(0, KV_ROW, MM_COLS):
        acc = _dot(xb, w_ref[:, c0:c0 + MM_COLS])
        for cb in range(MM_COLS // LANES):
            acc_ref[c0 // LANES + cb] = acc[:, cb * LANES:(cb + 1) * LANES]
        if dil == 1:
            kv_ref[0, 0, :, c0:c0 + MM_COLS] = acc.astype(kv_ref.dtype)
    if dil > 1:
        _deinterleave(kv_ref, acc_ref, dil)

    @pl.when(t >= first_tile)
    def _():
        tm = acc_ref.shape[1]
        for cb in range(KV_ROW // LANES):
            for rb in range(wt // LANES):
                r0 = tm - wt + rb * LANES
                kvt_ref[0, cb * LANES:(cb + 1) * LANES, rb * LANES:(rb + 1) * LANES] = (
                    acc_ref[cb, r0:r0 + LANES, :].T)


def _kv_proj(x, w, g, bsz, seq):
    win, dil = DIL_GROUPS[g]
    win = min(win, seq)
    tm = KV_ROWS
    tpb = seq // tm
    wt = min(win, tm)
    first_tile = tpb - win // wt
    assert tm % (dil * 2 * SUBLANES) == 0 and win % wt == 0 and seq % tm == 0
    k = x.shape[1]
    return pl.pallas_call(
        functools.partial(_kv_proj_kernel, dil=dil, first_tile=first_tile, wt=wt),
        grid=(bsz, tpb),
        in_specs=[pl.BlockSpec((tm, k), lambda b, t: (b * tpb + t, 0)),
                  pl.BlockSpec((k, KV_ROW), lambda b, t: (0, 0))],
        out_specs=[pl.BlockSpec((1, dil, tm // dil, KV_ROW), lambda b, t: (b, 0, t, 0)),
                   pl.BlockSpec((1, KV_ROW, wt), lambda b, t: (b, 0, jnp.maximum(t - first_tile, 0)))],
        out_shape=[jax.ShapeDtypeStruct((bsz, dil, seq // dil, KV_ROW), BF16),
                   jax.ShapeDtypeStruct((bsz, KV_ROW, win), F32)],
        scratch_shapes=[pltpu.VMEM((KV_ROW // LANES, tm, LANES), F32)],
        compiler_params=_cparams(2),
    )(x, w)


def _attn_prompt_kernel(slope_ref, q_ref, kv_ref, o_ref, lse_ref, *scratch, dil, has_prev, units):
    if has_prev:
        j = pl.program_id(2)
        kvp_ref, = scratch

        @pl.when(j == 0)
        def _():
            kvp_ref[...] = jnp.zeros_like(kvp_ref)

        rows = [pl.ds(u * Q_BLOCK, Q_BLOCK) for u in range(units)]
        for u in range(units):
            prev = kvp_ref if u == 0 else kv_ref.at[0, 0, rows[u - 1]]
            _attn_block(slope_ref, q_ref.at[0, 0, rows[u]], kv_ref.at[0, 0, rows[u]], prev,
                        o_ref.at[0, 0, rows[u]], lse_ref.at[0, 0, rows[u]], (j == 0) if u == 0 else None, dil)
        kvp_ref[...] = kv_ref[0, 0, rows[units - 1]]
    else:
        for u in range(units):
            _attn_block(slope_ref, q_ref.at[0, u], kv_ref.at[0, u], None, o_ref.at[0, u], lse_ref.at[0, u], None, dil)


def _attn_block(slope_ref, q_ref, kvc_ref, kvp_ref, o_ref, lse_ref, first, dil):
    has_prev = kvp_ref is not None
    qb = Q_BLOCK
    q = (q_ref[...].astype(F32) * ATT_SCALE).astype(BF16)
    row = lax.broadcasted_iota(jnp.int32, (qb, qb), 0)
    col = lax.broadcasted_iota(jnp.int32, (qb, qb), 1)
    lower = col <= row
    diag = col == row
    dist = (((row - col) & (qb - 1)) * dil).astype(F32)
    if not has_prev:
        dist = jnp.where(lower, dist, jnp.inf)
    elif first is not None:
        dist = jnp.where(jnp.logical_or(lower, jnp.logical_not(first)), dist, jnp.inf)
    far = float(qb * dil)
    lane = lax.broadcasted_iota(jnp.int32, (qb, LANES), 1)
    lane_lo = lane < ATT_HEAD_DIM
    mx_tile = jnp.zeros((qb, LANES), F32)
    den_tile = jnp.ones((qb, LANES), F32)

    for hb in range(0, ATT_HEADS, ATT_HEAD_BATCH):
        heads = list(range(hb, hb + ATT_HEAD_BATCH))
        kc, vc, kp, vp = {}, {}, {}, {}
        for pr in sorted({h // 2 for h in heads}):
            sl = slice(pr * LANES, (pr + 1) * LANES)
            vsl = slice(ATT_WIDTH + pr * LANES, ATT_WIDTH + (pr + 1) * LANES)
            kc[pr] = kvc_ref[:, sl]
            vc[pr] = kvc_ref[:, vsl]
            if has_prev:
                kp[pr] = kvp_ref[:, sl]
                vp[pr] = kvp_ref[:, vsl]
        qh = [jnp.where(lane_lo if h % 2 == 0 else jnp.logical_not(lane_lo),
                        q[:, (h // 2) * LANES:(h // 2 + 1) * LANES], jnp.zeros((qb, LANES), BF16)) for h in heads]
        slopes = [slope_ref[h] for h in heads]
        s_c = [_dot_nt(qh[i], kc[h // 2]) for i, h in enumerate(heads)]
        if has_prev:
            s_p = [_dot_nt(qh[i], kp[h // 2]) for i, h in enumerate(heads)]
            s = [jnp.where(lower, s_c[i], s_p[i]) - slopes[i] * dist for i in range(len(heads))]
            s_d = [jnp.sum(jnp.where(diag, s_p[i], 0.0), axis=-1, keepdims=True) - slopes[i] * far
                   for i in range(len(heads))]
            if first is not None:
                s_d = [jnp.where(first, -jnp.inf, sd) for sd in s_d]
            mx = [jnp.maximum(jnp.max(s[i], axis=-1, keepdims=True), s_d[i]) for i in range(len(heads))]
            p = [jnp.exp(s[i] - mx[i]) for i in range(len(heads))]
            p_d = [jnp.exp(s_d[i] - mx[i]) for i in range(len(heads))]
            den = [jnp.sum(p[i], axis=-1, keepdims=True) + p_d[i] for i in range(len(heads))]
            acc = [_dot(jnp.where(lower, p[i], 0.0).astype(BF16), vc[h // 2])
                   + _dot(jnp.where(lower, jnp.where(diag, p_d[i], 0.0), p[i]).astype(BF16), vp[h // 2])
                   for i, h in enumerate(heads)]
        else:
            s = [s_c[i] - slopes[i] * dist for i in range(len(heads))]
            mx = [jnp.max(s[i], axis=-1, keepdims=True) for i in range(len(heads))]
            p = [jnp.exp(s[i] - mx[i]) for i in range(len(heads))]
            den = [jnp.sum(p[i], axis=-1, keepdims=True) for i in range(len(heads))]
            acc = [_dot(p[i].astype(BF16), vc[h // 2]) for i, h in enumerate(heads)]
        out = [acc[i] * (1.0 / den[i]) for i in range(len(heads))]
        for i, h in enumerate(heads):
            mx_tile = jnp.where(lane == h, mx[i], mx_tile)
            den_tile = jnp.where(lane == h, den[i], den_tile)
        for i in range(0, len(heads), 2):
            pr = heads[i] // 2
            o_ref[:, pr * LANES:(pr + 1) * LANES] = jnp.where(lane_lo, out[i], out[i + 1]).astype(o_ref.dtype)
    lse_ref[...] = mx_tile + jnp.log(den_tile)


def _attn_prompt_group(q, kv, slopes, g):
    win, dil = DIL_GROUPS[g]
    bsz, _, n, _ = q.shape
    assert win // dil == Q_BLOCK and n % Q_BLOCK == 0
    nblk = n // Q_BLOCK
    has_prev = nblk > 1
    units = ATT_UNITS
    if has_prev:
        assert nblk % units == 0
        grid = (bsz, dil, nblk // units)
        blk = lambda width: pl.BlockSpec((1, 1, units * Q_BLOCK, width), lambda b, r, j: (b, r, j, 0))
    else:
        assert dil % units == 0
        grid = (bsz, dil // units, 1)
        blk = lambda width: pl.BlockSpec((1, units, Q_BLOCK, width), lambda b, r, j: (b, r, 0, 0))
    return pl.pallas_call(
        functools.partial(_attn_prompt_kernel, dil=dil, has_prev=has_prev, units=units),
        grid=grid,
        in_specs=[pl.BlockSpec(memory_space=pltpu.SMEM), blk(ATT_WIDTH), blk(KV_ROW)],
        out_specs=[blk(ATT_WIDTH), blk(LANES)],
        out_shape=[jax.ShapeDtypeStruct((bsz, dil, n, ATT_WIDTH), BF16),
                   jax.ShapeDtypeStruct((bsz, dil, n, LANES), F32)],
        scratch_shapes=[pltpu.VMEM((Q_BLOCK, KV_ROW), BF16)] if has_prev else [],
        compiler_params=_cparams(3),
    )(slopes, q, kv)


def _merge_out_kernel(o0_ref, o1_ref, o2_ref, l0_ref, l1_ref, l2_ref, gate_ref, e_ref, w_ref, r_ref,
                      g_ref, b_ref, out_ref, os_ref, ls_ref, lhs_ref, og_ref):
    for g, (o_ref, l_ref) in enumerate(((o0_ref, l0_ref), (o1_ref, l1_ref), (o2_ref, l2_ref))):
        dil = DIL_GROUPS[g][1]
        if dil > 1:
            _interleave(ls_ref.at[g - 1], l_ref, dil)
            _interleave(os_ref.at[g - 1], o_ref, dil)
    l0, l1, l2 = l0_ref[0, 0], ls_ref[0, 0], ls_ref[1, 0]
    top = jnp.maximum(jnp.maximum(l0, l1), l2)
    w0, w1, w2 = jnp.exp(l0 - top), jnp.exp(l1 - top), jnp.exp(l2 - top)
    inv = 1.0 / (w0 + w1 + w2)
    for i, wn in enumerate((w0 * inv, w1 * inv)):
        hi = wn.astype(BF16)
        lhs_ref[i, :, 0:LANES] = hi
        lhs_ref[i, :, LANES:2 * LANES] = (wn - hi.astype(F32)).astype(BF16)
    for c0 in range(0, ATT_WIDTH, MM_COLS):
        sl = slice(c0, c0 + MM_COLS)
        blocks = range(c0 // LANES, (c0 + MM_COLS) // LANES)
        w0e = _dot(lhs_ref[0], e_ref[:, sl])
        w1e = _dot(lhs_ref[1], e_ref[:, sl])
        o1 = jnp.concatenate([os_ref[0, cb] for cb in blocks], axis=1)
        o2 = jnp.concatenate([os_ref[1, cb] for cb in blocks], axis=1)
        o = w0e * o0_ref[0, 0, :, sl].astype(F32) + w1e * o1 + (1.0 - w0e - w1e) * o2
        og_ref[:, sl] = (o * _silu(gate_ref[:, sl])).astype(BF16)
    v = DEEPNORM_ALPHA * r_ref[...] + _dot(og_ref[...], w_ref[...])
    out_ref[...] = _layer_norm(v, g_ref[...], b_ref[...])


def _merge_out(os_, lses, gate, expand16, w, resid, g, b, seq):
    tm = PROJ_ROWS
    tpb = seq // tm
    m = resid.shape[0]
    rowblk = lambda width: pl.BlockSpec((tm, width), lambda i: (i, 0))
    const = lambda shape: pl.BlockSpec(shape, lambda i: (0, 0))
    resblk = lambda dil, width: pl.BlockSpec((1, dil, tm // dil, width), lambda i: (i // tpb, 0, i % tpb, 0))
    return pl.pallas_call(
        _merge_out_kernel,
        grid=(m // tm,),
        in_specs=[resblk(dil, ATT_WIDTH) for _, dil in DIL_GROUPS] + [resblk(dil, LANES) for _, dil in DIL_GROUPS]
                 + [rowblk(ATT_WIDTH), const((2 * LANES, ATT_WIDTH)), const((ATT_WIDTH, D_MODEL)), rowblk(D_MODEL),
                    const((1, D_MODEL)), const((1, D_MODEL))],
        out_specs=rowblk(D_MODEL),
        out_shape=jax.ShapeDtypeStruct((m, D_MODEL), F32),
        scratch_shapes=[pltpu.VMEM((N_DIL - 1, ATT_WIDTH // LANES, tm, LANES), F32),
                        pltpu.VMEM((N_DIL - 1, 1, tm, LANES), F32),
                        pltpu.VMEM((2, tm, 2 * LANES), BF16),
                        pltpu.VMEM((tm, ATT_WIDTH), BF16)],
        compiler_params=_cparams(1),
    )(*os_, *lses, gate, expand16, w, resid, g, b)


def _cache_shift_block(c_ref, sel_ref, o_ref, comp_ref, dil):
    x = c_ref[...]
    if dil > 1:
        comp_ref[...] = _dot(x.astype(BF16), sel_ref[...]).astype(BF16)
    else:
        comp_ref[...] = x.astype(BF16)
    o_ref[...] = pltpu.roll(x, x.shape[1] - 1, axis=1)


def _cache_rider(cache, g, steps):
    win, dil = DIL_GROUPS[g]
    nb = cache.shape[0]
    assert cache.shape[1] == win and win // dil == Q_BLOCK and (nb * KV_ROW) % (steps * SUBLANES) == 0
    rows = nb * KV_ROW // steps
    ct = cache.transpose(0, 2, 3, 4, 1).reshape(nb * KV_ROW, win)
    sel = (jnp.arange(win)[:, None] == jnp.arange(Q_BLOCK)[None, :] * dil).astype(BF16)
    return dict(dil=dil, ins=[ct, sel], rows=rows, win=win,
                out_shape=[jax.ShapeDtypeStruct((nb * KV_ROW, win), cache.dtype),
                           jax.ShapeDtypeStruct((nb * KV_ROW, Q_BLOCK), BF16)])


def _cache_append_kernel(c_ref, new_ref, o_ref):
    b = pl.program_id(0)
    nv = new_ref[...]
    pick = lax.broadcasted_iota(jnp.int32, nv.shape, 1) == b
    newcol = jnp.sum(jnp.where(pick, nv, 0.0), axis=1, keepdims=True)
    last = lax.broadcasted_iota(jnp.int32, c_ref.shape, 1) == LANES - 1
    o_ref[...] = jnp.where(last, newcol, c_ref[...])


def _cache_append(shifted, kvt_new, g, nb):
    win = shifted.shape[1]
    last_blk = win // LANES - 1
    out = pl.pallas_call(
        _cache_append_kernel,
        grid=(nb,),
        in_specs=[pl.BlockSpec((KV_ROW, LANES), lambda b: (b, last_blk)),
                  pl.BlockSpec((KV_ROW, nb), lambda b: (g, 0))],
        out_specs=pl.BlockSpec((KV_ROW, LANES), lambda b: (b, last_blk)),
        out_shape=jax.ShapeDtypeStruct(shifted.shape, shifted.dtype),
        input_output_aliases={0: 0},
        compiler_params=_cparams(1),
    )(shifted, kvt_new)
    return out.reshape(nb, 2, ATT_HEADS, ATT_HEAD_DIM, win).transpose(0, 4, 1, 2, 3)


def _attn_sample_kernel(slope_ref, q0_ref, q1_ref, q2_ref, gate_ref, n0_ref, n1_ref, n2_ref,
                        c0_ref, c1_ref, c2_ref, o_ref):
    nh = ATT_HEADS
    keys = Q_BLOCK
    lane = lax.broadcasted_iota(jnp.int32, (nh, ATT_WIDTH), 1)
    hrow = lax.broadcasted_iota(jnp.int32, (nh, ATT_WIDTH), 0)
    head_mask = (lane // ATT_HEAD_DIM) == hrow
    kidx = lax.broadcasted_iota(jnp.int32, (nh, keys), 1)
    slope = slope_ref[:, 0:1]
    outs, lses = [], []
    for g, (q_ref, n_ref, c_ref) in enumerate(((q0_ref, n0_ref, c0_ref), (q1_ref, n1_ref, c1_ref),
                                               (q2_ref, n2_ref, c2_ref))):
        dil = DIL_GROUPS[g][1]
        q = q_ref[0] * ATT_SCALE
        qm = jnp.where(head_mask, jnp.broadcast_to(q, (nh, ATT_WIDTH)), 0.0)
        new = n_ref[0]
        k_t = c_ref[0:ATT_WIDTH, :]
        v_t = c_ref[ATT_WIDTH:KV_ROW, :]
        dist = ((keys - kidx) * dil).astype(F32)
        s = _dot(qm.astype(BF16), k_t) - slope * dist
        s_new = jnp.sum(qm * new[:, 0:ATT_WIDTH], axis=-1, keepdims=True)
        mx = jnp.maximum(jnp.max(s, axis=-1, keepdims=True), s_new)
        p = jnp.exp(s - mx)
        p_new = jnp.exp(s_new - mx)
        den = jnp.sum(p, axis=-1, keepdims=True) + p_new
        outs.append((_dot_nt(p.astype(BF16), v_t) + p_new * new[:, ATT_WIDTH:]) / den)
        lses.append(mx + jnp.log(den))
    top = jnp.maximum(jnp.maximum(lses[0], lses[1]), lses[2])
    ws = [jnp.exp(l - top) for l in lses]
    o = (ws[0] * outs[0] + ws[1] * outs[1] + ws[2] * outs[2]) / (ws[0] + ws[1] + ws[2])
    o = jnp.sum(jnp.where(head_mask, o, 0.0), axis=0, keepdims=True)
    o_ref[0] = o * _silu(gate_ref[0])


def _attn_sample(proj, kv_new, comps, slopes_b):
    nb = proj.shape[0]
    proj3 = proj.reshape(nb, 1, 4 * ATT_WIDTH)
    kvn3 = kv_new.reshape(nb, 1, N_DIL * KV_ROW)
    qspec = lambda g: pl.BlockSpec((1, 1, ATT_WIDTH), lambda b: (b, 0, g))
    nspec = lambda g: pl.BlockSpec((1, 1, KV_ROW), lambda b: (b, 0, g))
    cspec = pl.BlockSpec((KV_ROW, Q_BLOCK), lambda b: (b, 0))
    o = pl.pallas_call(
        _attn_sample_kernel,
        grid=(nb,),
        in_specs=[pl.BlockSpec((ATT_HEADS, LANES), lambda b: (0, 0)),
                  qspec(0), qspec(1), qspec(2), qspec(3), nspec(0), nspec(1), nspec(2),
                  cspec, cspec, cspec],
        out_specs=pl.BlockSpec((1, 1, ATT_WIDTH), lambda b: (b, 0, 0)),
        out_shape=jax.ShapeDtypeStruct((nb, 1, ATT_WIDTH), F32),
        compiler_params=_cparams(1),
    )(slopes_b, proj3, proj3, proj3, proj3, kvn3, kvn3, kvn3, *comps)
    return o.reshape(nb, ATT_WIDTH)


def _pad_lanes(v):
    return jnp.pad(v.astype(F32), (0, LANES - v.shape[0])).reshape(1, LANES)


def kernel(x_prompt, x_sample, state_ssm, state_conv, cache_kv_w128, cache_kv_w512, cache_kv_w2048,
           a_in_proj, a_conv_w, a_conv_b, a_dt_bias, a_log, a_d, a_norm_w, a_out_proj,
           kv_proj, b_in_proj, b_out_proj, ln_g, ln_b):
    bsz, seq, _ = x_prompt.shape
    nb = x_sample.shape[0]
    assert x_sample.shape[1] == 1 and seq % CHUNK == 0
    caches = (cache_kv_w128, cache_kv_w512, cache_kv_w2048)

    heads = jnp.arange(LANES)[:, None]
    expand32 = (heads == jnp.arange(D_INNER)[None, :] // SSM_HEAD_DIM).astype(BF16)
    expand16 = (heads == jnp.arange(ATT_WIDTH)[None, :] // ATT_HEAD_DIM).astype(BF16)
    expand32 = jnp.concatenate([expand32, expand32], axis=0)
    expand16 = jnp.concatenate([expand16, expand16], axis=0)
    tril = (jnp.arange(CHUNK)[:, None] >= jnp.arange(CHUNK)[None, :]).astype(BF16)
    eye = jnp.eye(LANES, dtype=BF16)
    p = jnp.arange(CONV_WIDTH * CHUNK)
    blk = CONV_WIDTH * SUBLANES
    tap_i = (p // blk) * SUBLANES + p % SUBLANES
    tap_k = (p % blk) // SUBLANES
    shift = (jnp.arange(2 * CHUNK)[None, :] == (CHUNK + tap_i - (CONV_WIDTH - 1 - tap_k))[:, None]).astype(BF16)
    slopes =jnp.exp2(-8.0 * jnp.arange(1, ATT_HEADS + 1, dtype=F32) / ATT_HEADS)
    slopes_b = jnp.broadcast_to(slopes[:, None], (ATT_HEADS, LANES))

    hp = x_prompt.reshape(bsz * seq, D_MODEL)
    hs = x_sample.reshape(nb, D_MODEL)
    stacks_p = [jnp.zeros((N_A_LAYERS, bsz, SSM_HEADS, SSM_HEAD_DIM, SSM_STATE), F32),
                jnp.zeros((N_A_LAYERS, bsz, CONV_WIDTH - 1, CONV_DIM), F32)]
    stacks_s = [jnp.zeros(state_conv.shape, F32), jnp.zeros(state_ssm.shape, F32)]
    steps = bsz * (seq // CHUNK)
    riders = {0: [_cache_rider(caches[2], 2, steps)],
              1: [_cache_rider(caches[1], 1, steps), _cache_rider(caches[0], 0, steps)]}
    ride_groups = {0: [2], 1: [1, 0]}
    shifted, comps = [None] * N_DIL, [None] * N_DIL

    for i in range(N_A_LAYERS):
        w_in = a_in_proj[i]
        w_main = w_in[:, 0:D_INNER + CONV_DIM].astype(BF16)
        w_dt = jnp.pad(w_in[:, D_INNER + CONV_DIM:], ((0, 0), (0, LANES - SSM_HEADS))).astype(BF16)
        w_out = a_out_proj[i].astype(BF16)
        prm = dict(
            cwx=a_conv_w[i][:, 0:D_INNER], cwbc=a_conv_w[i][:, D_INNER:],
            cbx=a_conv_b[i][0:D_INNER].reshape(1, D_INNER), cbbc=a_conv_b[i][D_INNER:].reshape(1, BC_WIDTH),
            dtb=_pad_lanes(a_dt_bias[i]), alog=_pad_lanes(a_log[i]),
            dexp=jnp.repeat(a_d[i].astype(F32), SSM_HEAD_DIM).reshape(1, D_INNER),
            nw=a_norm_w[i].reshape(1, D_INNER), expand2=expand32, tril=tril, eye=eye, shift=shift)
        g_ln, b_ln = ln_g[i].reshape(1, D_MODEL), ln_b[i].reshape(1, D_MODEL)

        zx, dtp = _a_in_proj(hp, w_main, w_dt, BF16)
        yn, *rest = _ssd_prompt(zx, dtp, prm, bsz, seq, i, stacks_p, riders.get(i, []))
        stacks_p = rest[0:2]
        for n, g in enumerate(ride_groups.get(i, [])):
            shifted[g], comps[g] = rest[2 + 2 * n], rest[3 + 2 * n]
        hp = _matmul_ln(yn, w_out, hp, g_ln, b_ln)

        zx_s, dt_s = _a_in_proj(hs, w_main, w_dt, F32)
        yn_s, *stacks_s = _ssd_sample(zx_s, dt_s, state_conv, state_ssm, prm, i, stacks_s)
        hs = _matmul_ln(yn_s, w_out, hs, g_ln, b_ln)
    ssm_p, conv_p = stacks_p
    conv_s, ssm_s = stacks_s

    kvw = kv_proj.reshape(D_MODEL, 2, N_DIL, ATT_WIDTH)
    kvw_g = [jnp.concatenate([kvw[:, 0, g], kvw[:, 1, g]], axis=1).astype(BF16) for g in range(N_DIL)]
    kvw_all = jnp.concatenate(kvw_g, axis=1)
    kv_p, new_kv_p = [], []
    for g in range(N_DIL):
        kv_res, kv_t = _kv_proj(hp, kvw_g[g], g, bsz, seq)
        kv_p.append(kv_res)
        new_kv_p.append(kv_t.reshape(bsz, 2, ATT_HEADS, ATT_HEAD_DIM, kv_t.shape[-1]).transpose(0, 4, 1, 2, 3))
    kv_s = _matmul(hs, kvw_all, tn=1024)
    kvt_s = _matmul_nt(kvw_all.T, hs, tn=1024)
    new_kv_s = [_cache_append(shifted[g], kvt_s, g, nb) for g in range(N_DIL)]

    for j in range(N_B_LAYERS):
        layer = N_A_LAYERS + j
        w_in = b_in_proj[j].astype(BF16)
        w_out = b_out_proj[j].astype(BF16)
        g_ln, b_ln = ln_g[layer].reshape(1, D_MODEL), ln_b[layer].reshape(1, D_MODEL)

        q0, q1, q2, gate = _b_in_proj(hp, w_in, bsz, seq)
        res = [_attn_prompt_group(q, kv_p[g], slopes, g) for g, q in enumerate((q0, q1, q2))]
        hp = _merge_out([r[0] for r in res], [r[1] for r in res], gate, expand16, w_out, hp, g_ln, b_ln, seq)

        proj_s = _matmul(hs, w_in, tn=1024)
        og_s = _attn_sample(proj_s, kv_s, comps, slopes_b)
        hs = _matmul_ln(og_s, w_out, hs, g_ln, b_ln)

    return (hp.reshape(bsz, seq, D_MODEL), hs.reshape(nb, 1, D_MODEL),
            ssm_p, conv_p, new_kv_p[0], new_kv_p[1], new_kv_p[2],
            ssm_s, conv_s, new_kv_s[0], new_kv_s[1], new_kv_s[2])
```

```python
import functools

import jax
import jax.numpy as jnp
from jax import lax
from jax.experimental import pallas as pl
from jax.experimental.pallas import tpu as pltpu

F32 = jnp.float32
BF16 = jnp.bfloat16

D_MODEL = 1024
N_A_LAYERS = 2
N_B_LAYERS = 2
D_INNER = 2048
SSM_HEAD_DIM = 64
SSM_HEADS = 32
SSM_GROUPS = 4
SSM_STATE = 128
HEADS_PER_GROUP = SSM_HEADS // SSM_GROUPS
GROUP_WIDTH = D_INNER // SSM_GROUPS
CONV_WIDTH = 4
BC_WIDTH = 2 * SSM_GROUPS * SSM_STATE
CONV_DIM = D_INNER + BC_WIDTH
CHUNK = 128
DIL_GROUPS = ((128, 1), (512, 4), (2048, 16))
N_DIL = 3
ATT_HEADS = 16
ATT_HEAD_DIM = 64
ATT_WIDTH = 1024
KV_ROW = 2 * ATT_WIDTH
Q_BLOCK = 128
ATT_SCALE = ATT_HEAD_DIM ** -0.5
LN_EPS = 1e-5
RMS_EPS = 1e-5
DEEPNORM_ALPHA = (2.0 * 4) ** 0.25

LANES = 128
SUBLANES = 8
VMEM_LIMIT = 48 * 1024 * 1024

APPEND_SEQS = 2
ATT_UNITS = 4
ATT_HEAD_BATCH = 16
PROJ_ROWS = 512
PROJ_COLS = 512
MM_COLS = 256
KV_ROWS = 512


def _cparams(n_grid):
    return pltpu.CompilerParams(dimension_semantics=("arbitrary",) * n_grid, vmem_limit_bytes=VMEM_LIMIT)


def _silu(x):
    h = 0.5 * x
    return h + h * jnp.tanh(h)


def _softplus(x):
    return jnp.maximum(x, 0.0) + jnp.log1p(jnp.exp(-jnp.abs(x)))


def _split3(a):
    hi = a.astype(BF16)
    r1 = a - hi.astype(F32)
    mid = r1.astype(BF16)
    lo = (r1 - mid.astype(F32)).astype(BF16)
    return hi, mid, lo


def _dot(a, b):
    return jnp.dot(a, b, preferred_element_type=F32)


def _dot_nt(a, b):
    return lax.dot_general(a, b, (((1,), (1,)), ((), ())), preferred_element_type=F32)


def _expand_dot(a, m01_twice):
    hi = a.astype(BF16)
    lo = (a - hi.astype(F32)).astype(BF16)
    return _dot(jnp.concatenate([hi, lo], axis=1), m01_twice)


def _exact_dot_left(m01, a):
    hi, mid, lo = _split3(a)
    return _dot(m01, hi) + _dot(m01, mid) + _dot(m01, lo)


def _layer_norm(v, g, b):
    mu = jnp.mean(v, axis=-1, keepdims=True)
    d = v - mu
    var = jnp.mean(d * d, axis=-1, keepdims=True)
    return d * lax.rsqrt(var + LN_EPS) * g + b


def _lane_blocks(width):
    return [slice(cb * LANES, (cb + 1) * LANES) for cb in range(width // LANES)]


def _deinterleave(dst_ref, blk_ref, dil):
    rows = blk_ref.shape[1] // dil
    for r in range(dil):
        for cb, sl in enumerate(_lane_blocks(dst_ref.shape[-1])):
            dst_ref[0, r, :, sl] = blk_ref[cb, pl.ds(r, rows, stride=dil), :].astype(dst_ref.dtype)


def _interleave(blk_ref, src_ref, dil):
    rows = blk_ref.shape[1] // dil
    for r in range(dil):
        for cb, sl in enumerate(_lane_blocks(src_ref.shape[-1])):
            blk_ref[cb, pl.ds(r, rows, stride=dil), :] = src_ref[0, r, :, sl].astype(blk_ref.dtype)


def _mm_kernel(x_ref, w_ref, o_ref, xb_ref):
    @pl.when(pl.program_id(1) == 0)
    def _():
        xb_ref[...] = x_ref[...].astype(BF16)

    o_ref[...] = _dot(xb_ref[...], w_ref[...]).astype(o_ref.dtype)


def _matmul(x, w, *, tn, out_dtype=F32, tm=1024):
    m, k = x.shape
    n = w.shape[1]
    tm = min(tm, m)
    assert m % tm == 0 and n % tn == 0
    return pl.pallas_call(
        _mm_kernel,
        grid=(m // tm, n // tn),
        in_specs=[pl.BlockSpec((tm, k), lambda i, j: (i, 0)),
                  pl.BlockSpec((k, tn), lambda i, j: (0, j))],
        out_specs=pl.BlockSpec((tm, tn), lambda i, j: (i, j)),
        out_shape=jax.ShapeDtypeStruct((m, n), out_dtype),
        scratch_shapes=[pltpu.VMEM((tm, k), BF16)],
        compiler_params=_cparams(2),
    )(x, w)


def _a_in_proj_kernel(x_ref, w_ref, wdt_ref, zx_ref, dt_ref):
    xb = x_ref[...].astype(BF16)
    for c in range(w_ref.shape[1] // PROJ_COLS):
        sl = slice(c * PROJ_COLS, (c + 1) * PROJ_COLS)
        zx_ref[:, sl] = _dot(xb, w_ref[:, sl]).astype(zx_ref.dtype)
    dt_ref[...] = _dot(xb, wdt_ref[...])


def _a_in_proj(x, w_main, w_dt, out_dtype):
    m, k = x.shape
    n = w_main.shape[1]
    tm = min(PROJ_ROWS, m)
    assert m % tm == 0 and n % PROJ_COLS == 0
    return pl.pallas_call(
        _a_in_proj_kernel,
        grid=(m // tm,),
        in_specs=[pl.BlockSpec((tm, k), lambda i: (i, 0)),
                  pl.BlockSpec((k, n), lambda i: (0, 0)),
                  pl.BlockSpec((k, LANES), lambda i: (0, 0))],
        out_specs=[pl.BlockSpec((tm, n), lambda i: (i, 0)),
                   pl.BlockSpec((tm, LANES), lambda i: (i, 0))],
        out_shape=[jax.ShapeDtypeStruct((m, n), out_dtype),
                   jax.ShapeDtypeStruct((m, LANES), F32)],
        compiler_params=_cparams(1),
    )(x, w_main, w_dt)


def _mm_nt_kernel(w_ref, x_ref, o_ref):
    o_ref[...] = _dot_nt(w_ref[...], x_ref[...].astype(BF16))


def _matmul_nt(w_t, x, *, tn):
    n, k = w_t.shape
    m = x.shape[0]
    assert n % tn == 0
    return pl.pallas_call(
        _mm_nt_kernel,
        grid=(n // tn,),
        in_specs=[pl.BlockSpec((tn, k), lambda i: (i, 0)),
                  pl.BlockSpec((m, k), lambda i: (0, 0))],
        out_specs=pl.BlockSpec((tn, m), lambda i: (i, 0)),
        out_shape=jax.ShapeDtypeStruct((n, m), F32),
        compiler_params=_cparams(1),
    )(w_t, x)


def _mm_ln_kernel(y_ref, w_ref, r_ref, g_ref, b_ref, o_ref):
    acc = _dot(y_ref[...].astype(BF16), w_ref[...])
    v = DEEPNORM_ALPHA * r_ref[...] + acc
    o_ref[...] = _layer_norm(v, g_ref[...], b_ref[...])


def _matmul_ln(y, w, resid, g, b, *, tm=512):
    m, k = y.shape
    n = w.shape[1]
    tm = min(tm, m)
    assert m % tm == 0
    return pl.pallas_call(
        _mm_ln_kernel,
        grid=(m // tm,),
        in_specs=[pl.BlockSpec((tm, k), lambda i: (i, 0)),
                  pl.BlockSpec((k, n), lambda i: (0, 0)),
                  pl.BlockSpec((tm, n), lambda i: (i, 0)),
                  pl.BlockSpec((1, n), lambda i: (0, 0)),
                  pl.BlockSpec((1, n), lambda i: (0, 0))],
        out_specs=pl.BlockSpec((tm, n), lambda i: (i, 0)),
        out_shape=jax.ShapeDtypeStruct((m, n), F32),
        compiler_params=_cparams(1),
    )(y, w, resid, g, b)


def _ssd_prompt_kernel(z_ref, xr_ref, bcr_ref, dtr_ref, cwx_ref, cwbc_ref, cbx_ref, cbbc_ref,
                       dtb_ref, alog_ref, dexp_ref, nw_ref, e_ref, tril_ref, shift_ref,
                       y_ref, ssm_ref, conv_ref,
                       st_ref, extx_ref, extbc_ref, lhs_ref, bc_ref, xs_ref, xdtb_ref, xwb_ref, eae_ref,
                       cde_ref, yp_ref, *, ride):
    c = pl.program_id(1)
    t = CHUNK
    wide = 2 * LANES
    conv_ref = conv_ref.at[0, 0]

    @pl.when(c == 0)
    def _():
        st_ref[...] = jnp.zeros_like(st_ref)
        extx_ref[0:t, :] = jnp.zeros((t, D_INNER), BF16)
        extbc_ref[0:t, :] = jnp.zeros((t, BC_WIDTH), BF16)

    ride()

    dt = _softplus(dtr_ref[...] + dtb_ref[...])
    a = -jnp.exp(alog_ref[...])
    acs = _exact_dot_left(tril_ref[...], dt * a)
    acs_t = acs.T
    a_last = acs[t - 1:t, :]
    stacked = jnp.concatenate(
        [dt, jnp.exp(acs), jnp.exp(a_last - acs), jnp.broadcast_to(jnp.exp(a_last), (2 * SUBLANES, LANES))], axis=0)
    hi = stacked.astype(BF16)
    lhs_ref[:, 0:LANES] = hi
    lhs_ref[:, LANES:wide] = (stacked - hi.astype(F32)).astype(BF16)

    extx_ref[t:2 * t, :] = xr_ref[...]
    extbc_ref[t:2 * t, :] = bcr_ref[...]

    def conv(ext_ref, w_ref, b_ref, sl):
        taps = _dot(shift_ref[...], ext_ref[:, sl]).reshape(t // SUBLANES, CONV_WIDTH, SUBLANES, wide)
        acc = jnp.broadcast_to(b_ref[:, sl], (SUBLANES, wide))
        for k in range(CONV_WIDTH):
            acc = acc + taps[:, k] * w_ref[k:k + 1, sl]
        tail = taps[t // SUBLANES - 1, CONV_WIDTH - 1, SUBLANES - (CONV_WIDTH - 1):SUBLANES]
        return acc.reshape(t, wide), tail

    for cb in range(BC_WIDTH // wide):
        sl = slice(cb * wide, (cb + 1) * wide)
        acc, tail = conv(extbc_ref, cwbc_ref, cbbc_ref, sl)
        conv_ref[:, D_INNER + cb * wide:D_INNER + (cb + 1) * wide] = tail
        bc_ref[:, sl] = _silu(acc)
    for cb in range(D_INNER // wide):
        sl = slice(cb * wide, (cb + 1) * wide)
        acc, tail = conv(extx_ref, cwx_ref, cbx_ref, sl)
        conv_ref[:, sl] = tail
        xs = _silu(acc)
        ex = _dot(lhs_ref[...], e_ref[:, sl])
        xdt = xs * ex[0:t]
        xs_ref[:, sl] = xs
        xdtb_ref[:, sl] = xdt.astype(BF16)
        xwb_ref[:, sl] = (xdt * ex[2 * t:3 * t]).astype(BF16)
        eae_ref[:, sl] = ex[t:2 * t]
        cde_ref[:, sl] = ex[3 * t:3 * t + SUBLANES]
    extx_ref[0:t, :] = xr_ref[...]
    extbc_ref[0:t, :] = bcr_ref[...]

    row = lax.broadcasted_iota(jnp.int32, (t, t), 0)
    col = lax.broadcasted_iota(jnp.int32, (t, t), 1)
    causal = row >= col
    lane_lo = lax.broadcasted_iota(jnp.int32, (t, LANES), 1) < SSM_HEAD_DIM

    for g in range(SSM_GROUPS):
        g0 = g * GROUP_WIDTH
        gsl = slice(g0, g0 + GROUP_WIDTH)
        bg = bc_ref[:, g * SSM_STATE:(g + 1) * SSM_STATE]
        cg_b = bc_ref[:, (SSM_GROUPS + g) * SSM_STATE:(SSM_GROUPS + g + 1) * SSM_STATE].astype(BF16)
        cb = _dot_nt(cg_b, bg.astype(BF16))
        s_old = st_ref[:, gsl]
        yp_ref[:, gsl] = _dot(cg_b, s_old.astype(BF16)) * eae_ref[:, gsl]
        states = _dot(bg.T.astype(BF16), xwb_ref[:, gsl])
        st_ref[:, gsl] = s_old * cde_ref[0:1, gsl] + states
        sq = jnp.zeros((t, LANES), F32)
        for pr in range(HEADS_PER_GROUP // 2):
            sl = slice(g0 + pr * LANES, g0 + (pr + 1) * LANES)
            xp = xdtb_ref[:, sl]
            halves = []
            for half in range(2):
                h = g * HEADS_PER_GROUP + 2 * pr + half
                seg = acs[:, h:h + 1] - acs_t[h:h + 1, :]
                dec = jnp.exp(jnp.where(causal, seg, -jnp.inf))
                halves.append(_dot((cb * dec).astype(BF16), xp))
            y = jnp.where(lane_lo, halves[0], halves[1]) + yp_ref[:, sl] + xs_ref[:, sl] * dexp_ref[:, sl]
            hz = y * _silu(z_ref[:, sl].astype(F32))
            yp_ref[:, sl] = hz
            sq = sq + hz * hz
        scale = lax.rsqrt(jnp.sum(sq, axis=-1, keepdims=True) * (1.0 / GROUP_WIDTH) + RMS_EPS)
        y_ref[:, gsl] = (yp_ref[:, gsl] * scale * nw_ref[:, gsl]).astype(y_ref.dtype)

    @pl.when(c == pl.num_programs(1) - 1)
    def _():
        for j in range(D_INNER // LANES):
            tile = st_ref[:, j * LANES:(j + 1) * LANES].T
            ssm_ref[0, 0, 2 * j:2 * j + 2] = tile.reshape(2, SSM_HEAD_DIM, SSM_STATE)


def _skip_carried(body, n_in, n_carry):
    def kern(*refs):
        body(*refs[:n_in], *refs[n_in + n_carry:])
    return kern


def _ssd_prompt(zx, dtp, prm, bsz, seq, layer, carried, riders):
    nc = seq // CHUNK
    m = bsz * seq
    row = lambda b, c: b * nc + c
    const = lambda shape: pl.BlockSpec(shape, lambda b, c: (0,) * len(shape))
    ins = [zx, zx, zx, dtp, prm["cwx"], prm["cwbc"], prm["cbx"], prm["cbbc"], prm["dtb"], prm["alog"],
           prm["dexp"], prm["nw"], prm["expand2"], prm["tril"], prm["shift"]]
    extra = list(carried)
    n_in, n_ride = len(ins), 2 * len(riders)
    aliases = {n_in + n_ride: 1, n_in + n_ride + 1: 2}
    ride_in_specs, ride_out_specs, ride_ins, ride_out_shapes = [], [], [], []
    for rd in riders:
        blk = lambda width, rd=rd: pl.BlockSpec((rd["rows"], width), lambda b, c: (row(b, c), 0))
        ride_ins += rd["ins"]
        ride_in_specs += [blk(rd["win"]), const((rd["win"], Q_BLOCK))]
        ride_out_specs += [blk(rd["win"]), blk(Q_BLOCK)]
        ride_out_shapes += rd["out_shape"]

    def kern(*refs):
        out0 = n_in + n_ride + len(extra)
        ride_in, ride_out = refs[n_in:n_in + n_ride], refs[out0 + 3:out0 + 3 + n_ride]

        def ride():
            for i, rd in enumerate(riders):
                _cache_shift_block(ride_in[2 * i], ride_in[2 * i + 1], ride_out[2 * i], ride_out[2 * i + 1], rd["dil"])

        _ssd_prompt_kernel(*refs[:n_in], *refs[out0:out0 + 3], *refs[out0 + 3 + n_ride:], ride=ride)

    return pl.pallas_call(
        kern,
        grid=(bsz, nc),
        in_specs=[pl.BlockSpec((CHUNK, D_INNER), lambda b, c: (row(b, c), 0)),
                  pl.BlockSpec((CHUNK, D_INNER), lambda b, c: (row(b, c), 1)),
                  pl.BlockSpec((CHUNK, BC_WIDTH), lambda b, c: (row(b, c), 4)),
                  pl.BlockSpec((CHUNK, LANES), lambda b, c: (row(b, c), 0)),
                  const((CONV_WIDTH, D_INNER)), const((CONV_WIDTH, BC_WIDTH)),
                  const((1, D_INNER)), const((1, BC_WIDTH)),
                  const((1, LANES)), const((1, LANES)),
                  const((1, D_INNER)), const((1, D_INNER)),
                  const((2 * LANES, D_INNER)), const((CHUNK, CHUNK)), const((CONV_WIDTH * CHUNK, 2 * CHUNK))]
                 + ride_in_specs + [pl.BlockSpec(memory_space=pl.ANY)] * len(extra),
        out_specs=[pl.BlockSpec((CHUNK, D_INNER), lambda b, c: (row(b, c), 0)),
                   pl.BlockSpec((1, 1, SSM_HEADS, SSM_HEAD_DIM, SSM_STATE), lambda b, c: (layer, b, 0, 0, 0)),
                   pl.BlockSpec((1, 1, CONV_WIDTH - 1, CONV_DIM), lambda b, c: (layer, b, 0, 0))]
                  + ride_out_specs,
        out_shape=[jax.ShapeDtypeStruct((m, D_INNER), BF16),
                   jax.ShapeDtypeStruct((N_A_LAYERS, bsz, SSM_HEADS, SSM_HEAD_DIM, SSM_STATE), F32),
                   jax.ShapeDtypeStruct((N_A_LAYERS, bsz, CONV_WIDTH - 1, CONV_DIM), F32)]
                  + ride_out_shapes,
        scratch_shapes=[pltpu.VMEM((SSM_STATE, D_INNER), F32),
                        pltpu.VMEM((2 * CHUNK, D_INNER), BF16),
                        pltpu.VMEM((2 * CHUNK, BC_WIDTH), BF16),
                        pltpu.VMEM((3 * CHUNK + 2 * SUBLANES, 2 * LANES), BF16),
                        pltpu.VMEM((CHUNK, BC_WIDTH), F32),
                        pltpu.VMEM((CHUNK, D_INNER), F32),
                        pltpu.VMEM((CHUNK, D_INNER), BF16),
                        pltpu.VMEM((CHUNK, D_INNER), BF16),
                        pltpu.VMEM((CHUNK, D_INNER), F32),
                        pltpu.VMEM((SUBLANES, D_INNER), F32),
                        pltpu.VMEM((CHUNK, D_INNER), F32)],
        input_output_aliases=aliases,
        compiler_params=_cparams(2),
    )(*ins, *ride_ins, *extra)


def _ssd_sample_kernel(z_ref, xr_ref, bcr_ref, dtr_ref, conv_ref, ssm_ref,
                       cwx_ref, cwbc_ref, cbx_ref, cbbc_ref, dtb_ref, alog_ref, dexp_ref, nw_ref,
                       e_ref, eye_ref,
                       y_ref, convo_ref, ssmo_ref, st_ref):
    prev = conv_ref[0, 0]
    raw = jnp.concatenate([xr_ref[0], bcr_ref[0]], axis=1)
    cw = jnp.concatenate([cwx_ref[...], cwbc_ref[...]], axis=1)
    cbias = jnp.concatenate([cbx_ref[...], cbbc_ref[...]], axis=1)
    acc = cbias + raw * cw[CONV_WIDTH - 1:CONV_WIDTH, :]
    for k in range(CONV_WIDTH - 1):
        acc = acc + prev[k:k + 1, :] * cw[k:k + 1, :]
    convo_ref[0, 0, 0:CONV_WIDTH - 2, :] = prev[1:CONV_WIDTH - 1, :]
    convo_ref[0, 0, CONV_WIDTH - 2:CONV_WIDTH - 1, :] = raw
    xbc = _silu(acc)
    xs = xbc[:, 0:D_INNER]
    bm = xbc[:, D_INNER:D_INNER + SSM_GROUPS * SSM_STATE]
    cm = xbc[:, D_INNER + SSM_GROUPS * SSM_STATE:]

    dt = _softplus(dtr_ref[0] + dtb_ref[...])
    dec = jnp.exp(dt * -jnp.exp(alog_ref[...]))
    stacked = jnp.concatenate([dt, dec, jnp.zeros((SUBLANES - 2, LANES), F32)], axis=0)
    ex = _expand_dot(stacked, e_ref[...])
    dt_e = ex[0:1]
    dec_e = ex[1:2]
    xdt = xs * dt_e

    rows = jnp.concatenate([bm[:, g * SSM_STATE:(g + 1) * SSM_STATE] for g in range(SSM_GROUPS)]
                           + [cm[:, g * SSM_STATE:(g + 1) * SSM_STATE] for g in range(SSM_GROUPS)], axis=0)
    hi, mid, lo = _split3(rows)
    eye = eye_ref[...]
    cols = _dot_nt(eye, hi) + _dot_nt(eye, mid) + _dot_nt(eye, lo)

    for j in range(D_INNER // LANES):
        tile = ssm_ref[0, 0, 2 * j:2 * j + 2].reshape(LANES, SSM_STATE)
        st_ref[:, j * LANES:(j + 1) * LANES] = tile.T
    y_groups = []
    for g in range(SSM_GROUPS):
        g0 = g * GROUP_WIDTH
        new = (st_ref[:, g0:g0 + GROUP_WIDTH] * dec_e[:, g0:g0 + GROUP_WIDTH]
               + cols[:, g:g + 1] * xdt[:, g0:g0 + GROUP_WIDTH])
        st_ref[:, g0:g0 + GROUP_WIDTH] = new
        y_groups.append(jnp.sum(new * cols[:, SSM_GROUPS + g:SSM_GROUPS + g + 1], axis=0, keepdims=True))
    for j in range(D_INNER // LANES):
        tile = st_ref[:, j * LANES:(j + 1) * LANES].T
        ssmo_ref[0, 0, 2 * j:2 * j + 2] = tile.reshape(2, SSM_HEAD_DIM, SSM_STATE)
    y = jnp.concatenate(y_groups, axis=1) + xs * dexp_ref[...]

    hz = y * _silu(z_ref[0])
    normed = []
    for g in range(SSM_GROUPS):
        hg = hz[:, g * GROUP_WIDTH:(g + 1) * GROUP_WIDTH]
        normed.append(hg * lax.rsqrt(jnp.mean(hg * hg, axis=-1, keepdims=True) + RMS_EPS))
    y_ref[0] = jnp.concatenate(normed, axis=1) * nw_ref[...]


def _ssd_sample(zx, dts, conv_state, ssm_state, prm, layer, carried):
    nb = zx.shape[0]
    zx3 = zx.reshape(nb, 1, zx.shape[1])
    dt3 = dts.reshape(nb, 1, LANES)
    const = lambda shape: pl.BlockSpec(shape, lambda b: (0,) * len(shape))
    conv_blk = pl.BlockSpec((1, 1, CONV_WIDTH - 1, CONV_DIM), lambda b: (layer, b, 0, 0))
    ssm_blk = pl.BlockSpec((1, 1, SSM_HEADS, SSM_HEAD_DIM, SSM_STATE), lambda b: (layer, b, 0, 0, 0))
    ins = [zx3, zx3, zx3, dt3, conv_state, ssm_state, prm["cwx"], prm["cwbc"], prm["cbx"], prm["cbbc"],
           prm["dtb"], prm["alog"], prm["dexp"], prm["nw"], prm["expand2"], prm["eye"]]
    extra = list(carried)
    aliases = {len(ins): 1, len(ins) + 1: 2}
    y, conv_new, ssm_new = pl.pallas_call(
        _skip_carried(_ssd_sample_kernel, len(ins), len(extra)),
        grid=(nb,),
        in_specs=[pl.BlockSpec((1, 1, D_INNER), lambda b: (b, 0, 0)),
                  pl.BlockSpec((1, 1, D_INNER), lambda b: (b, 0, 1)),
                  pl.BlockSpec((1, 1, BC_WIDTH), lambda b: (b, 0, 4)),
                  pl.BlockSpec((1, 1, LANES), lambda b: (b, 0, 0)),
                  conv_blk, ssm_blk,
                  const((CONV_WIDTH, D_INNER)), const((CONV_WIDTH, BC_WIDTH)),
                  const((1, D_INNER)), const((1, BC_WIDTH)),
                  const((1, LANES)), const((1, LANES)),
                  const((1, D_INNER)), const((1, D_INNER)),
                  const((2 * LANES, D_INNER)), const((LANES, LANES))]
                 + [pl.BlockSpec(memory_space=pl.ANY)] * len(extra),
        out_specs=[pl.BlockSpec((1, 1, D_INNER), lambda b: (b, 0, 0)), conv_blk, ssm_blk],
        out_shape=[jax.ShapeDtypeStruct((nb, 1, D_INNER), F32),
                   jax.ShapeDtypeStruct(conv_state.shape, F32),
                   jax.ShapeDtypeStruct(ssm_state.shape, F32)],
        scratch_shapes=[pltpu.VMEM((SSM_STATE, D_INNER), F32)],
        input_output_aliases=aliases,
        compiler_params=_cparams(1),
    )(*ins, *extra)
    return y.reshape(nb, D_INNER), conv_new, ssm_new


def _b_in_proj_kernel(x_ref, w_ref, q0_ref, q1_ref, q2_ref, gate_ref, acc_ref):
    xb = x_ref[...].astype(BF16)
    for g, dst in enumerate((q0_ref, q1_ref, q2_ref)):
        dil = DIL_GROUPS[g][1]
        for c0 in range(0, ATT_WIDTH, MM_COLS):
            acc = _dot(xb, w_ref[:, g * ATT_WIDTH + c0:g * ATT_WIDTH + c0 + MM_COLS])
            if dil == 1:
                dst[0, 0, :, c0:c0 + MM_COLS] = acc.astype(dst.dtype)
            else:
                for cb in range(MM_COLS // LANES):
                    acc_ref[c0 // LANES + cb] = acc[:, cb * LANES:(cb + 1) * LANES]
        if dil > 1:
            _deinterleave(dst, acc_ref, dil)
    for c0 in range(0, ATT_WIDTH, PROJ_COLS):
        gate_ref[:, c0:c0 + PROJ_COLS] = _dot(xb, w_ref[:, N_DIL * ATT_WIDTH + c0:N_DIL * ATT_WIDTH + c0 + PROJ_COLS])


def _b_in_proj(x, w, bsz, seq):
    tm = PROJ_ROWS
    tpb = seq // tm
    m, k = x.shape
    qspecs, qshapes = [], []
    for _, dil in DIL_GROUPS:
        assert tm % (dil * 2 * SUBLANES) == 0
        qspecs.append(pl.BlockSpec((1, dil, tm // dil, ATT_WIDTH), lambda i: (i // tpb, 0, i % tpb, 0)))
        qshapes.append(jax.ShapeDtypeStruct((bsz, dil, seq // dil, ATT_WIDTH), BF16))
    return pl.pallas_call(
        _b_in_proj_kernel,
        grid=(m // tm,),
        in_specs=[pl.BlockSpec((tm, k), lambda i: (i, 0)),
                  pl.BlockSpec(w.shape, lambda i: (0, 0))],
        out_specs=qspecs + [pl.BlockSpec((tm, ATT_WIDTH), lambda i: (i, 0))],
        out_shape=qshapes + [jax.ShapeDtypeStruct((m, ATT_WIDTH), F32)],
        scratch_shapes=[pltpu.VMEM((ATT_WIDTH // LANES, tm, LANES), F32)],
        compiler_params=_cparams(1),
    )(x, w)


def _kv_proj_kernel(x_ref, w_ref, kv_ref, kvt_ref, acc_ref, *, dil, first_tile, wt):
    t = pl.program_id(1)
    xb = x_ref[...].astype(BF16)
    for c0 in range(0, KV_ROW, MM_COLS):
        acc = _dot(xb, w_ref[:, c0:c0 + MM_COLS])
        for cb in range(MM_COLS // LANES):
            acc_ref[c0 // LANES + cb] = acc[:, cb * LANES:(cb + 1) * LANES]
        if dil == 1:
            kv_ref[0, 0, :, c0:c0 + MM_COLS] = acc.astype(kv_ref.dtype)
    if dil > 1:
        _deinterleave(kv_ref, acc_ref, dil)

    @pl.when(t >= first_tile)
    def _():
        tm = acc_ref.shape[1]
        for cb in range(KV_ROW // LANES):
            for rb in range(wt // LANES):
                r0 = tm - wt + rb * LANES
                kvt_ref[0, cb * LANES:(cb + 1) * LANES, rb * LANES:(rb + 1) * LANES] = (
                    acc_ref[cb, r0:r0 + LANES, :].T)


def _kv_proj(x, w, g, bsz, seq):
    win, dil = DIL_GROUPS[g]
    win = min(win, seq)
    tm = KV_ROWS
    tpb = seq // tm
    wt = min(win, tm)
    first_tile = tpb - win // wt
    assert tm % (dil * 2 * SUBLANES) == 0 and win % wt == 0 and seq % tm == 0
    k = x.shape[1]
    return pl.pallas_call(
        functools.partial(_kv_proj_kernel, dil=dil, first_tile=first_tile, wt=wt),
        grid=(bsz, tpb),
        in_specs=[pl.BlockSpec((tm, k), lambda b, t: (b * tpb + t, 0)),
                  pl.BlockSpec((k, KV_ROW), lambda b, t: (0, 0))],
        out_specs=[pl.BlockSpec((1, dil, tm // dil, KV_ROW), lambda b, t: (b, 0, t, 0)),
                   pl.BlockSpec((1, KV_ROW, wt), lambda b, t: (b, 0, jnp.maximum(t - first_tile, 0)))],
        out_shape=[jax.ShapeDtypeStruct((bsz, dil, seq // dil, KV_ROW), BF16),
                   jax.ShapeDtypeStruct((bsz, KV_ROW, win), F32)],
        scratch_shapes=[pltpu.VMEM((KV_ROW // LANES, tm, LANES), F32)],
        compiler_params=_cparams(2),
    )(x, w)


def _attn_prompt_kernel(slope_ref, q_ref, kv_ref, o_ref, lse_ref, *scratch, dil, has_prev, units):
    if has_prev:
        j = pl.program_id(2)
        kvp_ref, = scratch

        @pl.when(j == 0)
        def _():
            kvp_ref[...] = jnp.zeros_like(kvp_ref)

        rows = [pl.ds(u * Q_BLOCK, Q_BLOCK) for u in range(units)]
        for u in range(units):
            prev = kvp_ref if u == 0 else kv_ref.at[0, 0, rows[u - 1]]
            _attn_block(slope_ref, q_ref.at[0, 0, rows[u]], kv_ref.at[0, 0, rows[u]], prev,
                        o_ref.at[0, 0, rows[u]], lse_ref.at[0, 0, rows[u]], (j == 0) if u == 0 else None, dil)
        kvp_ref[...] = kv_ref[0, 0, rows[units - 1]]
    else:
        for u in range(units):
            _attn_block(slope_ref, q_ref.at[0, u], kv_ref.at[0, u], None, o_ref.at[0, u], lse_ref.at[0, u], None, dil)


def _attn_block(slope_ref, q_ref, kvc_ref, kvp_ref, o_ref, lse_ref, first, dil):
    has_prev = kvp_ref is not None
    qb = Q_BLOCK
    q = (q_ref[...].astype(F32) * ATT_SCALE).astype(BF16)
    row = lax.broadcasted_iota(jnp.int32, (qb, qb), 0)
    col = lax.broadcasted_iota(jnp.int32, (qb, qb), 1)
    lower = col <= row
    diag = col == row
    dist = (((row - col) & (qb - 1)) * dil).astype(F32)
    if not has_prev:
        dist = jnp.where(lower, dist, jnp.inf)
    elif first is not None:
        dist = jnp.where(jnp.logical_or(lower, jnp.logical_not(first)), dist, jnp.inf)
    far = float(qb * dil)
    lane = lax.broadcasted_iota(jnp.int32, (qb, LANES), 1)
    lane_lo = lane < ATT_HEAD_DIM
    mx_tile = jnp.zeros((qb, LANES), F32)
    den_tile = jnp.ones((qb, LANES), F32)

    for hb in range(0, ATT_HEADS, ATT_HEAD_BATCH):
        heads = list(range(hb, hb + ATT_HEAD_BATCH))
        kc, vc, kp, vp = {}, {}, {}, {}
        for pr in sorted({h // 2 for h in heads}):
            sl = slice(pr * LANES, (pr + 1) * LANES)
            vsl = slice(ATT_WIDTH + pr * LANES, ATT_WIDTH + (pr + 1) * LANES)
            kc[pr] = kvc_ref[:, sl]
            vc[pr] = kvc_ref[:, vsl]
            if has_prev:
                kp[pr] = kvp_ref[:, sl]
                vp[pr] = kvp_ref[:, vsl]
        qh = [jnp.where(lane_lo if h % 2 == 0 else jnp.logical_not(lane_lo),
                        q[:, (h // 2) * LANES:(h // 2 + 1) * LANES], jnp.zeros((qb, LANES), BF16)) for h in heads]
        slopes = [slope_ref[h] for h in heads]
        s_c = [_dot_nt(qh[i], kc[h // 2]) for i, h in enumerate(heads)]
        if has_prev:
            s_p = [_dot_nt(qh[i], kp[h // 2]) for i, h in enumerate(heads)]
            s = [jnp.where(lower, s_c[i], s_p[i]) - slopes[i] * dist for i in range(len(heads))]
            s_d = [jnp.sum(jnp.where(diag, s_p[i], 0.0), axis=-1, keepdims=True) - slopes[i] * far
                   for i in range(len(heads))]
            if first is not None:
                s_d = [jnp.where(first, -jnp.inf, sd) for sd in s_d]
            mx = [jnp.maximum(jnp.max(s[i], axis=-1, keepdims=True), s_d[i]) for i in range(len(heads))]
            p = [jnp.exp(s[i] - mx[i]) for i in range(len(heads))]
            p_d = [jnp.exp(s_d[i] - mx[i]) for i in range(len(heads))]
            den = [jnp.sum(p[i], axis=-1, keepdims=True) + p_d[i] for i in range(len(heads))]
            acc = [_dot(jnp.where(lower, p[i], 0.0).astype(BF16), vc[h // 2])
                   + _dot(jnp.where(lower, jnp.where(diag, p_d[i], 0.0), p[i]).astype(BF16), vp[h // 2])
                   for i, h in enumerate(heads)]
        else:
            s = [s_c[i] - slopes[i] * dist for i in range(len(heads))]
            mx = [jnp.max(s[i], axis=-1, keepdims=True) for i in range(len(heads))]
            p = [jnp.exp(s[i] - mx[i]) for i in range(len(heads))]
            den = [jnp.sum(p[i], axis=-1, keepdims=True) for i in range(len(heads))]
            acc = [_dot(p[i].astype(BF16), vc[h // 2]) for i, h in enumerate(heads)]
        out = [acc[i] * (1.0 / den[i]) for i in range(len(heads))]
        for i, h in enumerate(heads):
            mx_tile = jnp.where(lane == h, mx[i], mx_tile)
            den_tile = jnp.where(lane == h, den[i], den_tile)
        for i in range(0, len(heads), 2):
            pr = heads[i] // 2
            o_ref[:, pr * LANES:(pr + 1) * LANES] = jnp.where(lane_lo, out[i], out[i + 1]).astype(o_ref.dtype)
    lse_ref[...] = mx_tile + jnp.log(den_tile)


def _attn_prompt_group(q, kv, slopes, g):
    win, dil = DIL_GROUPS[g]
    bsz, _, n, _ = q.shape
    assert win // dil == Q_BLOCK and n % Q_BLOCK == 0
    nblk = n // Q_BLOCK
    has_prev = nblk > 1
    units = ATT_UNITS
    if has_prev:
        assert nblk % units == 0
        grid = (bsz, dil, nblk // units)
        blk = lambda width: pl.BlockSpec((1, 1, units * Q_BLOCK, width), lambda b, r, j: (b, r, j, 0))
    else:
        assert dil % units == 0
        grid = (bsz, dil // units, 1)
        blk = lambda width: pl.BlockSpec((1, units, Q_BLOCK, width), lambda b, r, j: (b, r, 0, 0))
    return pl.pallas_call(
        functools.partial(_attn_prompt_kernel, dil=dil, has_prev=has_prev, units=units),
        grid=grid,
        in_specs=[pl.BlockSpec(memory_space=pltpu.SMEM), blk(ATT_WIDTH), blk(KV_ROW)],
        out_specs=[blk(ATT_WIDTH), blk(LANES)],
        out_shape=[jax.ShapeDtypeStruct((bsz, dil, n, ATT_WIDTH), BF16),
                   jax.ShapeDtypeStruct((bsz, dil, n, LANES), F32)],
        scratch_shapes=[pltpu.VMEM((Q_BLOCK, KV_ROW), BF16)] if has_prev else [],
        compiler_params=_cparams(3),
    )(slopes, q, kv)


def _merge_out_kernel(o0_ref, o1_ref, o2_ref, l0_ref, l1_ref, l2_ref, gate_ref, e_ref, w_ref, r_ref,
                      g_ref, b_ref, out_ref, os_ref, ls_ref, lhs_ref, og_ref):
    for g, (o_ref, l_ref) in enumerate(((o0_ref, l0_ref), (o1_ref, l1_ref), (o2_ref, l2_ref))):
        dil = DIL_GROUPS[g][1]
        if dil > 1:
            _interleave(ls_ref.at[g - 1], l_ref, dil)
            _interleave(os_ref.at[g - 1], o_ref, dil)
    l0, l1, l2 = l0_ref[0, 0], ls_ref[0, 0], ls_ref[1, 0]
    top = jnp.maximum(jnp.maximum(l0, l1), l2)
    w0, w1, w2 = jnp.exp(l0 - top), jnp.exp(l1 - top), jnp.exp(l2 - top)
    inv = 1.0 / (w0 + w1 + w2)
    for i, wn in enumerate((w0 * inv, w1 * inv)):
        hi = wn.astype(BF16)
        lhs_ref[i, :, 0:LANES] = hi
        lhs_ref[i, :, LANES:2 * LANES] = (wn - hi.astype(F32)).astype(BF16)
    for c0 in range(0, ATT_WIDTH, MM_COLS):
        sl = slice(c0, c0 + MM_COLS)
        blocks = range(c0 // LANES, (c0 + MM_COLS) // LANES)
        w0e = _dot(lhs_ref[0], e_ref[:, sl])
        w1e = _dot(lhs_ref[1], e_ref[:, sl])
        o1 = jnp.concatenate([os_ref[0, cb] for cb in blocks], axis=1)
        o2 = jnp.concatenate([os_ref[1, cb] for cb in blocks], axis=1)
        o = w0e * o0_ref[0, 0, :, sl].astype(F32) + w1e * o1 + (1.0 - w0e - w1e) * o2
        og_ref[:, sl] = (o * _silu(gate_ref[:, sl])).astype(BF16)
    v = DEEPNORM_ALPHA * r_ref[...] + _dot(og_ref[...], w_ref[...])
    out_ref[...] = _layer_norm(v, g_ref[...], b_ref[...])


def _merge_out(os_, lses, gate, expand16, w, resid, g, b, seq):
    tm = PROJ_ROWS
    tpb = seq // tm
    m = resid.shape[0]
    rowblk = lambda width: pl.BlockSpec((tm, width), lambda i: (i, 0))
    const = lambda shape: pl.BlockSpec(shape, lambda i: (0, 0))
    resblk = lambda dil, width: pl.BlockSpec((1, dil, tm // dil, width), lambda i: (i // tpb, 0, i % tpb, 0))
    return pl.pallas_call(
        _merge_out_kernel,
        grid=(m // tm,),
        in_specs=[resblk(dil, ATT_WIDTH) for _, dil in DIL_GROUPS] + [resblk(dil, LANES) for _, dil in DIL_GROUPS]
                 + [rowblk(ATT_WIDTH), const((2 * LANES, ATT_WIDTH)), const((ATT_WIDTH, D_MODEL)), rowblk(D_MODEL),
                    const((1, D_MODEL)), const((1, D_MODEL))],
        out_specs=rowblk(D_MODEL),
        out_shape=jax.ShapeDtypeStruct((m, D_MODEL), F32),
        scratch_shapes=[pltpu.VMEM((N_DIL - 1, ATT_WIDTH // LANES, tm, LANES), F32),
                        pltpu.VMEM((N_DIL - 1, 1, tm, LANES), F32),
                        pltpu.VMEM((2, tm, 2 * LANES), BF16),
                        pltpu.VMEM((tm, ATT_WIDTH), BF16)],
        compiler_params=_cparams(1),
    )(*os_, *lses, gate, expand16, w, resid, g, b)


def _cache_shift_block(c_ref, sel_ref, o_ref, comp_ref, dil):
    x = c_ref[...]
    if dil > 1:
        comp_ref[...] = _dot(x.astype(BF16), sel_ref[...]).astype(BF16)
    else:
        comp_ref[...] = x.astype(BF16)
    o_ref[...] = pltpu.roll(x, x.shape[1] - 1, axis=1)


def _cache_rider(cache, g, steps):
    win, dil = DIL_GROUPS[g]
    nb = cache.shape[0]
    assert cache.shape[1] == win and win // dil == Q_BLOCK and (nb * KV_ROW) % (steps * SUBLANES) == 0
    rows = nb * KV_ROW // steps
    ct = cache.transpose(0, 2, 3, 4, 1).reshape(nb * KV_ROW, win)
    sel = (jnp.arange(win)[:, None] == jnp.arange(Q_BLOCK)[None, :] * dil).astype(BF16)
    return dict(dil=dil, ins=[ct, sel], rows=rows, win=win,
                out_shape=[jax.ShapeDtypeStruct((nb * KV_ROW, win), cache.dtype),
                           jax.ShapeDtypeStruct((nb * KV_ROW, Q_BLOCK), BF16)])


def _cache_append_kernel(*refs):
    new_ref = refs[N_DIL]
    first = pl.program_id(0) * APPEND_SEQS
    batch = lax.broadcasted_iota(jnp.int32, (KV_ROW, new_ref.shape[1]), 1)
    last = lax.broadcasted_iota(jnp.int32, (KV_ROW, LANES), 1) == LANES - 1
    for g in range(N_DIL):
        c_ref, o_ref = refs[g], refs[N_DIL + 1 + g]
        nv = new_ref[g * KV_ROW:(g + 1) * KV_ROW, :]
        for s in range(APPEND_SEQS):
            newcol = jnp.sum(jnp.where(batch == first + s, nv, 0.0), axis=1, keepdims=True)
            rows = slice(s * KV_ROW, (s + 1) * KV_ROW)
            o_ref[rows, :] = jnp.where(last, newcol, c_ref[rows, :])


def _cache_append(shifted, kvt_new, nb):
    assert nb % APPEND_SEQS == 0
    blocks = [pl.BlockSpec((APPEND_SEQS * KV_ROW, LANES), lambda b, s=s: (b, s.shape[1] // LANES - 1))
              for s in shifted]
    outs = pl.pallas_call(
        _cache_append_kernel,
        grid=(nb // APPEND_SEQS,),
        in_specs=blocks + [pl.BlockSpec(kvt_new.shape, lambda b: (0, 0))],
        out_specs=blocks,
        out_shape=[jax.ShapeDtypeStruct(s.shape, s.dtype) for s in shifted],
        input_output_aliases={g: g for g in range(N_DIL)},
        compiler_params=_cparams(1),
    )(*shifted, kvt_new)
    return [o.reshape(nb, 2, ATT_HEADS, ATT_HEAD_DIM, o.shape[1]).transpose(0, 4, 1, 2, 3) for o in outs]


def _attn_sample_kernel(slope_ref, q0_ref, q1_ref, q2_ref, gate_ref, n0_ref, n1_ref, n2_ref,
                        c0_ref, c1_ref, c2_ref, o_ref):
    nh = ATT_HEADS
    keys = Q_BLOCK
    lane = lax.broadcasted_iota(jnp.int32, (nh, ATT_WIDTH), 1)
    hrow = lax.broadcasted_iota(jnp.int32, (nh, ATT_WIDTH), 0)
    head_mask = (lane // ATT_HEAD_DIM) == hrow
    kidx = lax.broadcasted_iota(jnp.int32, (nh, keys), 1)
    slope = slope_ref[:, 0:1]
    outs, lses = [], []
    for g, (q_ref, n_ref, c_ref) in enumerate(((q0_ref, n0_ref, c0_ref), (q1_ref, n1_ref, c1_ref),
                                               (q2_ref, n2_ref, c2_ref))):
        dil = DIL_GROUPS[g][1]
        q = q_ref[0] * ATT_SCALE
        qm = jnp.where(head_mask, jnp.broadcast_to(q, (nh, ATT_WIDTH)), 0.0)
        new = n_ref[0]
        k_t = c_ref[0:ATT_WIDTH, :]
        v_t = c_ref[ATT_WIDTH:KV_ROW, :]
        dist = ((keys - kidx) * dil).astype(F32)
        s = _dot(qm.astype(BF16), k_t) - slope * dist
        s_new = jnp.sum(qm * new[:, 0:ATT_WIDTH], axis=-1, keepdims=True)
        mx = jnp.maximum(jnp.max(s, axis=-1, keepdims=True), s_new)
        p = jnp.exp(s - mx)
        p_new = jnp.exp(s_new - mx)
        den = jnp.sum(p, axis=-1, keepdims=True) + p_new
        outs.append((_dot_nt(p.astype(BF16), v_t) + p_new * new[:, ATT_WIDTH:]) / den)
        lses.append(mx + jnp.log(den))
    top = jnp.maximum(jnp.maximum(lses[0], lses[1]), lses[2])
    ws = [jnp.exp(l - top) for l in lses]
    o = (ws[0] * outs[0] + ws[1] * outs[1] + ws[2] * outs[2]) / (ws[0] + ws[1] + ws[2])
    o = jnp.sum(jnp.where(head_mask, o, 0.0), axis=0, keepdims=True)
    o_ref[0] = o * _silu(gate_ref[0])


def _attn_sample(proj, kv_new, comps, slopes_b):
    nb = proj.shape[0]
    proj3 = proj.reshape(nb, 1, 4 * ATT_WIDTH)
    kvn3 = kv_new.reshape(nb, 1, N_DIL * KV_ROW)
    qspec = lambda g: pl.BlockSpec((1, 1, ATT_WIDTH), lambda b: (b, 0, g))
    nspec = lambda g: pl.BlockSpec((1, 1, KV_ROW), lambda b: (b, 0, g))
    cspec = pl.BlockSpec((KV_ROW, Q_BLOCK), lambda b: (b, 0))
    o = pl.pallas_call(
        _attn_sample_kernel,
        grid=(nb,),
        in_specs=[pl.BlockSpec((ATT_HEADS, LANES), lambda b: (0, 0)),
                  qspec(0), qspec(1), qspec(2), qspec(3), nspec(0), nspec(1), nspec(2),
                  cspec, cspec, cspec],
        out_specs=pl.BlockSpec((1, 1, ATT_WIDTH), lambda b: (b, 0, 0)),
        out_shape=jax.ShapeDtypeStruct((nb, 1, ATT_WIDTH), F32),
        compiler_params=_cparams(1),
    )(slopes_b, proj3, proj3, proj3, proj3, kvn3, kvn3, kvn3, *comps)
    return o.reshape(nb, ATT_WIDTH)


def _pad_lanes(v):
    return jnp.pad(v.astype(F32), (0, LANES - v.shape[0])).reshape(1, LANES)


def kernel(x_prompt, x_sample, state_ssm, state_conv, cache_kv_w128, cache_kv_w512, cache_kv_w2048,
           a_in_proj, a_conv_w, a_conv_b, a_dt_bias, a_log, a_d, a_norm_w, a_out_proj,
           kv_proj, b_in_proj, b_out_proj, ln_g, ln_b):
    bsz, seq, _ = x_prompt.shape
    nb = x_sample.shape[0]
    assert x_sample.shape[1] == 1 and seq % CHUNK == 0
    caches = (cache_kv_w128, cache_kv_w512, cache_kv_w2048)

    heads = jnp.arange(LANES)[:, None]
    expand32 = (heads == jnp.arange(D_INNER)[None, :] // SSM_HEAD_DIM).astype(BF16)
    expand16 = (heads == jnp.arange(ATT_WIDTH)[None, :] // ATT_HEAD_DIM).astype(BF16)
    expand32 = jnp.concatenate([expand32, expand32], axis=0)
    expand16 = jnp.concatenate([expand16, expand16], axis=0)
    tril = (jnp.arange(CHUNK)[:, None] >= jnp.arange(CHUNK)[None, :]).astype(BF16)
    eye = jnp.eye(LANES, dtype=BF16)
    p = jnp.arange(CONV_WIDTH * CHUNK)
    blk = CONV_WIDTH * SUBLANES
    tap_i = (p // blk) * SUBLANES + p % SUBLANES
    tap_k = (p % blk) // SUBLANES
    shift = (jnp.arange(2 * CHUNK)[None, :] == (CHUNK + tap_i - (CONV_WIDTH - 1 - tap_k))[:, None]).astype(BF16)
    slopes =jnp.exp2(-8.0 * jnp.arange(1, ATT_HEADS + 1, dtype=F32) / ATT_HEADS)
    slopes_b = jnp.broadcast_to(slopes[:, None], (ATT_HEADS, LANES))

    hp = x_prompt.reshape(bsz * seq, D_MODEL)
    hs = x_sample.reshape(nb, D_MODEL)
    stacks_p = [jnp.zeros((N_A_LAYERS, bsz, SSM_HEADS, SSM_HEAD_DIM, SSM_STATE), F32),
                jnp.zeros((N_A_LAYERS, bsz, CONV_WIDTH - 1, CONV_DIM), F32)]
    stacks_s = [jnp.zeros(state_conv.shape, F32), jnp.zeros(state_ssm.shape, F32)]
    steps = bsz * (seq // CHUNK)
    riders = {0: [_cache_rider(caches[2], 2, steps)],
              1: [_cache_rider(caches[1], 1, steps), _cache_rider(caches[0], 0, steps)]}
    ride_groups = {0: [2], 1: [1, 0]}
    shifted, comps = [None] * N_DIL, [None] * N_DIL

    for i in range(N_A_LAYERS):
        w_in = a_in_proj[i]
        w_main = w_in[:, 0:D_INNER + CONV_DIM].astype(BF16)
        w_dt = jnp.pad(w_in[:, D_INNER + CONV_DIM:], ((0, 0), (0, LANES - SSM_HEADS))).astype(BF16)
        w_out = a_out_proj[i].astype(BF16)
        prm = dict(
            cwx=a_conv_w[i][:, 0:D_INNER], cwbc=a_conv_w[i][:, D_INNER:],
            cbx=a_conv_b[i][0:D_INNER].reshape(1, D_INNER), cbbc=a_conv_b[i][D_INNER:].reshape(1, BC_WIDTH),
            dtb=_pad_lanes(a_dt_bias[i]), alog=_pad_lanes(a_log[i]),
            dexp=jnp.repeat(a_d[i].astype(F32), SSM_HEAD_DIM).reshape(1, D_INNER),
            nw=a_norm_w[i].reshape(1, D_INNER), expand2=expand32, tril=tril, eye=eye, shift=shift)
        g_ln, b_ln = ln_g[i].reshape(1, D_MODEL), ln_b[i].reshape(1, D_MODEL)

        zx, dtp = _a_in_proj(hp, w_main, w_dt, BF16)
        yn, *rest = _ssd_prompt(zx, dtp, prm, bsz, seq, i, stacks_p, riders.get(i, []))
        stacks_p = rest[0:2]
        for n, g in enumerate(ride_groups.get(i, [])):
            shifted[g], comps[g] = rest[2 + 2 * n], rest[3 + 2 * n]
        hp = _matmul_ln(yn, w_out, hp, g_ln, b_ln)

        zx_s, dt_s = _a_in_proj(hs, w_main, w_dt, F32)
        yn_s, *stacks_s = _ssd_sample(zx_s, dt_s, state_conv, state_ssm, prm, i, stacks_s)
        hs = _matmul_ln(yn_s, w_out, hs, g_ln, b_ln)
    ssm_p, conv_p = stacks_p
    conv_s, ssm_s = stacks_s

    kvw = kv_proj.reshape(D_MODEL, 2, N_DIL, ATT_WIDTH)
    kvw_g = [jnp.concatenate([kvw[:, 0, g], kvw[:, 1, g]], axis=1).astype(BF16) for g in range(N_DIL)]
    kvw_all = jnp.concatenate(kvw_g, axis=1)
    kv_p, new_kv_p = [], []
    for g in range(N_DIL):
        kv_res, kv_t = _kv_proj(hp, kvw_g[g], g, bsz, seq)
        kv_p.append(kv_res)
        new_kv_p.append(kv_t.reshape(bsz, 2, ATT_HEADS, ATT_HEAD_DIM, kv_t.shape[-1]).transpose(0, 4, 1, 2, 3))
    kv_s = _matmul(hs, kvw_all, tn=1024)
    kvt_s = _matmul_nt(kvw_all.T, hs, tn=1024)
    new_kv_s = _cache_append(shifted, kvt_s, nb)

    for j in range(N_B_LAYERS):
        layer = N_A_LAYERS + j
        w_in = b_in_proj[j].astype(BF16)
        w_out = b_out_proj[j].astype(BF16)
        g_ln, b_ln = ln_g[layer].reshape(1, D_MODEL), ln_b[layer].reshape(1, D_MODEL)

        q0, q1, q2, gate = _b_in_proj(hp, w_in, bsz, seq)
        res = [_attn_prompt_group(q, kv_p[g], slopes, g) for g, q in enumerate((q0, q1, q2))]
        hp = _merge_out([r[0] for r in res], [r[1] for r in res], gate, expand16, w_out, hp, g_ln, b_ln, seq)

        proj_s = _matmul(hs, w_in, tn=1024)
        og_s = _attn_sample(proj_s, kv_s, comps, slopes_b)
        hs = _matmul_ln(og_s, w_out, hs, g_ln, b_ln)

    return (hp.reshape(bsz, seq, D_MODEL), hs.reshape(nb, 1, D_MODEL),
            ssm_p, conv_p, new_kv_p[0], new_kv_p[1], new_kv_p[2],
            ssm_s, conv_s, new_kv_s[0], new_kv_s[1], new_kv_s[2])
```

```python
import functools

import jax
import jax.numpy as jnp
from jax import lax
from jax.experimental import pallas as pl
from jax.experimental.pallas import tpu as pltpu

F32 = jnp.float32
BF16 = jnp.bfloat16

D_MODEL = 1024
N_A_LAYERS = 2
N_B_LAYERS = 2
D_INNER = 2048
SSM_HEAD_DIM = 64
SSM_HEADS = 32
SSM_GROUPS = 4
SSM_STATE = 128
HEADS_PER_GROUP = SSM_HEADS // SSM_GROUPS
GROUP_WIDTH = D_INNER // SSM_GROUPS
CONV_WIDTH = 4
BC_WIDTH = 2 * SSM_GROUPS * SSM_STATE
CONV_DIM = D_INNER + BC_WIDTH
CHUNK = 128
DIL_GROUPS = ((128, 1), (512, 4), (2048, 16))
N_DIL = 3
ATT_HEADS = 16
ATT_HEAD_DIM = 64
ATT_WIDTH = 1024
KV_ROW = 2 * ATT_WIDTH
Q_BLOCK = 128
ATT_SCALE = ATT_HEAD_DIM ** -0.5
LN_EPS = 1e-5
RMS_EPS = 1e-5
DEEPNORM_ALPHA = (2.0 * 4) ** 0.25

LANES = 128
SUBLANES = 8
VMEM_LIMIT = 48 * 1024 * 1024

APPEND_SEQS = 2
ATT_UNITS = 8
ATT_HEAD_BATCH = 16
PROJ_ROWS = 512
PROJ_COLS = 512
MM_COLS = 256
KV_ROWS = 512


def _cparams(n_grid):
    return pltpu.CompilerParams(dimension_semantics=("arbitrary",) * n_grid, vmem_limit_bytes=VMEM_LIMIT)


def _silu(x):
    h = 0.5 * x
    return h + h * jnp.tanh(h)


def _softplus(x):
    return jnp.maximum(x, 0.0) + jnp.log1p(jnp.exp(-jnp.abs(x)))


def _split3(a):
    hi = a.astype(BF16)
    r1 = a - hi.astype(F32)
    mid = r1.astype(BF16)
    lo = (r1 - mid.astype(F32)).astype(BF16)
    return hi, mid, lo


def _dot(a, b):
    return jnp.dot(a, b, preferred_element_type=F32)


def _dot_nt(a, b):
    return lax.dot_general(a, b, (((1,), (1,)), ((), ())), preferred_element_type=F32)


def _expand_dot(a, m01_twice):
    hi = a.astype(BF16)
    lo = (a - hi.astype(F32)).astype(BF16)
    return _dot(jnp.concatenate([hi, lo], axis=1), m01_twice)


def _exact_dot_left(m01, a):
    hi, mid, lo = _split3(a)
    return _dot(m01, hi) + _dot(m01, mid) + _dot(m01, lo)


def _layer_norm(v, g, b):
    mu = jnp.mean(v, axis=-1, keepdims=True)
    d = v - mu
    var = jnp.mean(d * d, axis=-1, keepdims=True)
    return d * lax.rsqrt(var + LN_EPS) * g + b


def _lane_blocks(width):
    return [slice(cb * LANES, (cb + 1) * LANES) for cb in range(width // LANES)]


def _deinterleave(dst_ref, blk_ref, dil):
    rows = blk_ref.shape[1] // dil
    for r in range(dil):
        for cb, sl in enumerate(_lane_blocks(dst_ref.shape[-1])):
            dst_ref[0, r, :, sl] = blk_ref[cb, pl.ds(r, rows, stride=dil), :].astype(dst_ref.dtype)


def _interleave(blk_ref, src_ref, dil):
    rows = blk_ref.shape[1] // dil
    for r in range(dil):
        for cb, sl in enumerate(_lane_blocks(src_ref.shape[-1])):
            blk_ref[cb, pl.ds(r, rows, stride=dil), :] = src_ref[0, r, :, sl].astype(blk_ref.dtype)


def _mm_kernel(x_ref, w_ref, o_ref, xb_ref):
    @pl.when(pl.program_id(1) == 0)
    def _():
        xb_ref[...] = x_ref[...].astype(BF16)

    o_ref[...] = _dot(xb_ref[...], w_ref[...]).astype(o_ref.dtype)


def _matmul(x, w, *, tn, out_dtype=F32, tm=1024):
    m, k = x.shape
    n = w.shape[1]
    tm = min(tm, m)
    assert m % tm == 0 and n % tn == 0
    return pl.pallas_call(
        _mm_kernel,
        grid=(m // tm, n // tn),
        in_specs=[pl.BlockSpec((tm, k), lambda i, j: (i, 0)),
                  pl.BlockSpec((k, tn), lambda i, j: (0, j))],
        out_specs=pl.BlockSpec((tm, tn), lambda i, j: (i, j)),
        out_shape=jax.ShapeDtypeStruct((m, n), out_dtype),
        scratch_shapes=[pltpu.VMEM((tm, k), BF16)],
        compiler_params=_cparams(2),
    )(x, w)


def _a_in_proj_kernel(x_ref, w_ref, wdt_ref, zx_ref, dt_ref):
    xb = x_ref[...].astype(BF16)
    for c in range(w_ref.shape[1] // PROJ_COLS):
        sl = slice(c * PROJ_COLS, (c + 1) * PROJ_COLS)
        zx_ref[:, sl] = _dot(xb, w_ref[:, sl]).astype(zx_ref.dtype)
    dt_ref[...] = _dot(xb, wdt_ref[...])


def _a_in_proj(x, w_main, w_dt, out_dtype):
    m, k = x.shape
    n = w_main.shape[1]
    tm = min(PROJ_ROWS, m)
    assert m % tm == 0 and n % PROJ_COLS == 0
    return pl.pallas_call(
        _a_in_proj_kernel,
        grid=(m // tm,),
        in_specs=[pl.BlockSpec((tm, k), lambda i: (i, 0)),
                  pl.BlockSpec((k, n), lambda i: (0, 0)),
                  pl.BlockSpec((k, LANES), lambda i: (0, 0))],
        out_specs=[pl.BlockSpec((tm, n), lambda i: (i, 0)),
                   pl.BlockSpec((tm, LANES), lambda i: (i, 0))],
        out_shape=[jax.ShapeDtypeStruct((m, n), out_dtype),
                   jax.ShapeDtypeStruct((m, LANES), F32)],
        compiler_params=_cparams(1),
    )(x, w_main, w_dt)


def _mm_nt_kernel(w_ref, x_ref, o_ref):
    o_ref[...] = _dot_nt(w_ref[...], x_ref[...].astype(BF16))


def _matmul_nt(w_t, x, *, tn):
    n, k = w_t.shape
    m = x.shape[0]
    assert n % tn == 0
    return pl.pallas_call(
        _mm_nt_kernel,
        grid=(n // tn,),
        in_specs=[pl.BlockSpec((tn, k), lambda i: (i, 0)),
                  pl.BlockSpec((m, k), lambda i: (0, 0))],
        out_specs=pl.BlockSpec((tn, m), lambda i: (i, 0)),
        out_shape=jax.ShapeDtypeStruct((n, m), F32),
        compiler_params=_cparams(1),
    )(w_t, x)


def _mm_ln_kernel(y_ref, w_ref, r_ref, g_ref, b_ref, o_ref):
    acc = _dot(y_ref[...].astype(BF16), w_ref[...])
    v = DEEPNORM_ALPHA * r_ref[...] + acc
    o_ref[...] = _layer_norm(v, g_ref[...], b_ref[...])


def _matmul_ln(y, w, resid, g, b, *, tm=512):
    m, k = y.shape
    n = w.shape[1]
    tm = min(tm, m)
    assert m % tm == 0
    return pl.pallas_call(
        _mm_ln_kernel,
        grid=(m // tm,),
        in_specs=[pl.BlockSpec((tm, k), lambda i: (i, 0)),
                  pl.BlockSpec((k, n), lambda i: (0, 0)),
                  pl.BlockSpec((tm, n), lambda i: (i, 0)),
                  pl.BlockSpec((1, n), lambda i: (0, 0)),
                  pl.BlockSpec((1, n), lambda i: (0, 0))],
        out_specs=pl.BlockSpec((tm, n), lambda i: (i, 0)),
        out_shape=jax.ShapeDtypeStruct((m, n), F32),
        compiler_params=_cparams(1),
    )(y, w, resid, g, b)


def _ssd_prompt_kernel(z_ref, xr_ref, bcr_ref, dtr_ref, cwx_ref, cwbc_ref, cbx_ref, cbbc_ref,
                       dtb_ref, alog_ref, dexp_ref, nw_ref, e_ref, tril_ref, shift_ref,
                       y_ref, ssm_ref, conv_ref,
                       st_ref, extx_ref, extbc_ref, lhs_ref, bc_ref, xs_ref, xdtb_ref, xwb_ref, eae_ref,
                       cde_ref, yp_ref, *, ride):
    c = pl.program_id(1)
    t = CHUNK
    wide = 2 * LANES
    conv_ref = conv_ref.at[0, 0]

    @pl.when(c == 0)
    def _():
        st_ref[...] = jnp.zeros_like(st_ref)
        extx_ref[0:t, :] = jnp.zeros((t, D_INNER), BF16)
        extbc_ref[0:t, :] = jnp.zeros((t, BC_WIDTH), BF16)

    ride()

    dt = _softplus(dtr_ref[...] + dtb_ref[...])
    a = -jnp.exp(alog_ref[...])
    acs = _exact_dot_left(tril_ref[...], dt * a)
    acs_t = acs.T
    a_last = acs[t - 1:t, :]
    stacked = jnp.concatenate(
        [dt, jnp.exp(acs), jnp.exp(a_last - acs), jnp.broadcast_to(jnp.exp(a_last), (2 * SUBLANES, LANES))], axis=0)
    hi = stacked.astype(BF16)
    lhs_ref[:, 0:LANES] = hi
    lhs_ref[:, LANES:wide] = (stacked - hi.astype(F32)).astype(BF16)

    extx_ref[t:2 * t, :] = xr_ref[...]
    extbc_ref[t:2 * t, :] = bcr_ref[...]

    def conv(ext_ref, w_ref, b_ref, sl):
        taps = _dot(shift_ref[...], ext_ref[:, sl]).reshape(t // SUBLANES, CONV_WIDTH, SUBLANES, wide)
        acc = jnp.broadcast_to(b_ref[:, sl], (SUBLANES, wide))
        for k in range(CONV_WIDTH):
            acc = acc + taps[:, k] * w_ref[k:k + 1, sl]
        tail = taps[t // SUBLANES - 1, CONV_WIDTH - 1, SUBLANES - (CONV_WIDTH - 1):SUBLANES]
        return acc.reshape(t, wide), tail

    for cb in range(BC_WIDTH // wide):
        sl = slice(cb * wide, (cb + 1) * wide)
        acc, tail = conv(extbc_ref, cwbc_ref, cbbc_ref, sl)
        conv_ref[:, D_INNER + cb * wide:D_INNER + (cb + 1) * wide] = tail
        bc_ref[:, sl] = _silu(acc)
    for cb in range(D_INNER // wide):
        sl = slice(cb * wide, (cb + 1) * wide)
        acc, tail = conv(extx_ref, cwx_ref, cbx_ref, sl)
        conv_ref[:, sl] = tail
        xs = _silu(acc)
        ex = _dot(lhs_ref[...], e_ref[:, sl])
        xdt = xs * ex[0:t]
        xs_ref[:, sl] = xs
        xdtb_ref[:, sl] = xdt.astype(BF16)
        xwb_ref[:, sl] = (xdt * ex[2 * t:3 * t]).astype(BF16)
        eae_ref[:, sl] = ex[t:2 * t]
        cde_ref[:, sl] = ex[3 * t:3 * t + SUBLANES]
    extx_ref[0:t, :] = xr_ref[...]
    extbc_ref[0:t, :] = bcr_ref[...]

    row = lax.broadcasted_iota(jnp.int32, (t, t), 0)
    col = lax.broadcasted_iota(jnp.int32, (t, t), 1)
    causal = row >= col
    lane_lo = lax.broadcasted_iota(jnp.int32, (t, LANES), 1) < SSM_HEAD_DIM

    for g in range(SSM_GROUPS):
        g0 = g * GROUP_WIDTH
        gsl = slice(g0, g0 + GROUP_WIDTH)
        bg = bc_ref[:, g * SSM_STATE:(g + 1) * SSM_STATE]
        cg_b = bc_ref[:, (SSM_GROUPS + g) * SSM_STATE:(SSM_GROUPS + g + 1) * SSM_STATE].astype(BF16)
        cb = _dot_nt(cg_b, bg.astype(BF16))
        s_old = st_ref[:, gsl]
        yp_ref[:, gsl] = _dot(cg_b, s_old.astype(BF16)) * eae_ref[:, gsl]
        states = _dot(bg.T.astype(BF16), xwb_ref[:, gsl])
        st_ref[:, gsl] = s_old * cde_ref[0:1, gsl] + states
        sq = jnp.zeros((t, LANES), F32)
        for pr in range(HEADS_PER_GROUP // 2):
            sl = slice(g0 + pr * LANES, g0 + (pr + 1) * LANES)
            xp = xdtb_ref[:, sl]
            halves = []
            for half in range(2):
                h = g * HEADS_PER_GROUP + 2 * pr + half
                seg = acs[:, h:h + 1] - acs_t[h:h + 1, :]
                dec = jnp.exp(jnp.where(causal, seg, -jnp.inf))
                halves.append(_dot((cb * dec).astype(BF16), xp))
            y = jnp.where(lane_lo, halves[0], halves[1]) + yp_ref[:, sl] + xs_ref[:, sl] * dexp_ref[:, sl]
            hz = y * _silu(z_ref[:, sl].astype(F32))
            yp_ref[:, sl] = hz
            sq = sq + hz * hz
        scale = lax.rsqrt(jnp.sum(sq, axis=-1, keepdims=True) * (1.0 / GROUP_WIDTH) + RMS_EPS)
        y_ref[:, gsl] = (yp_ref[:, gsl] * scale * nw_ref[:, gsl]).astype(y_ref.dtype)

    @pl.when(c == pl.num_programs(1) - 1)
    def _():
        for j in range(D_INNER // LANES):
            tile = st_ref[:, j * LANES:(j + 1) * LANES].T
            ssm_ref[0, 0, 2 * j:2 * j + 2] = tile.reshape(2, SSM_HEAD_DIM, SSM_STATE)


def _skip_carried(body, n_in, n_carry):
    def kern(*refs):
        body(*refs[:n_in], *refs[n_in + n_carry:])
    return kern


def _ssd_prompt(zx, dtp, prm, bsz, seq, layer, carried, riders):
    nc = seq // CHUNK
    m = bsz * seq
    row = lambda b, c: b * nc + c
    const = lambda shape: pl.BlockSpec(shape, lambda b, c: (0,) * len(shape))
    ins = [zx, zx, zx, dtp, prm["cwx"], prm["cwbc"], prm["cbx"], prm["cbbc"], prm["dtb"], prm["alog"],
           prm["dexp"], prm["nw"], prm["expand2"], prm["tril"], prm["shift"]]
    extra = list(carried)
    n_in, n_ride = len(ins), 2 * len(riders)
    aliases = {n_in + n_ride: 1, n_in + n_ride + 1: 2}
    ride_in_specs, ride_out_specs, ride_ins, ride_out_shapes = [], [], [], []
    for rd in riders:
        blk = lambda width, rd=rd: pl.BlockSpec((rd["rows"], width), lambda b, c: (row(b, c), 0))
        ride_ins += rd["ins"]
        ride_in_specs += [blk(rd["win"]), const((rd["win"], Q_BLOCK))]
        ride_out_specs += [blk(rd["win"]), blk(Q_BLOCK)]
        ride_out_shapes += rd["out_shape"]

    def kern(*refs):
        out0 = n_in + n_ride + len(extra)
        ride_in, ride_out = refs[n_in:n_in + n_ride], refs[out0 + 3:out0 + 3 + n_ride]

        def ride():
            for i, rd in enumerate(riders):
                _cache_shift_block(ride_in[2 * i], ride_in[2 * i + 1], ride_out[2 * i], ride_out[2 * i + 1], rd["dil"])

        _ssd_prompt_kernel(*refs[:n_in], *refs[out0:out0 + 3], *refs[out0 + 3 + n_ride:], ride=ride)

    return pl.pallas_call(
        kern,
        grid=(bsz, nc),
        in_specs=[pl.BlockSpec((CHUNK, D_INNER), lambda b, c: (row(b, c), 0)),
                  pl.BlockSpec((CHUNK, D_INNER), lambda b, c: (row(b, c), 1)),
                  pl.BlockSpec((CHUNK, BC_WIDTH), lambda b, c: (row(b, c), 4)),
                  pl.BlockSpec((CHUNK, LANES), lambda b, c: (row(b, c), 0)),
                  const((CONV_WIDTH, D_INNER)), const((CONV_WIDTH, BC_WIDTH)),
                  const((1, D_INNER)), const((1, BC_WIDTH)),
                  const((1, LANES)), const((1, LANES)),
                  const((1, D_INNER)), const((1, D_INNER)),
                  const((2 * LANES, D_INNER)), const((CHUNK, CHUNK)), const((CONV_WIDTH * CHUNK, 2 * CHUNK))]
                 + ride_in_specs + [pl.BlockSpec(memory_space=pl.ANY)] * len(extra),
        out_specs=[pl.BlockSpec((CHUNK, D_INNER), lambda b, c: (row(b, c), 0)),
                   pl.BlockSpec((1, 1, SSM_HEADS, SSM_HEAD_DIM, SSM_STATE), lambda b, c: (layer, b, 0, 0, 0)),
                   pl.BlockSpec((1, 1, CONV_WIDTH - 1, CONV_DIM), lambda b, c: (layer, b, 0, 0))]
                  + ride_out_specs,
        out_shape=[jax.ShapeDtypeStruct((m, D_INNER), BF16),
                   jax.ShapeDtypeStruct((N_A_LAYERS, bsz, SSM_HEADS, SSM_HEAD_DIM, SSM_STATE), F32),
                   jax.ShapeDtypeStruct((N_A_LAYERS, bsz, CONV_WIDTH - 1, CONV_DIM), F32)]
                  + ride_out_shapes,
        scratch_shapes=[pltpu.VMEM((SSM_STATE, D_INNER), F32),
                        pltpu.VMEM((2 * CHUNK, D_INNER), BF16),
                        pltpu.VMEM((2 * CHUNK, BC_WIDTH), BF16),
                        pltpu.VMEM((3 * CHUNK + 2 * SUBLANES, 2 * LANES), BF16),
                        pltpu.VMEM((CHUNK, BC_WIDTH), F32),
                        pltpu.VMEM((CHUNK, D_INNER), F32),
                        pltpu.VMEM((CHUNK, D_INNER), BF16),
                        pltpu.VMEM((CHUNK, D_INNER), BF16),
                        pltpu.VMEM((CHUNK, D_INNER), F32),
                        pltpu.VMEM((SUBLANES, D_INNER), F32),
                        pltpu.VMEM((CHUNK, D_INNER), F32)],
        input_output_aliases=aliases,
        compiler_params=_cparams(2),
    )(*ins, *ride_ins, *extra)


def _ssd_sample_kernel(z_ref, xr_ref, bcr_ref, dtr_ref, conv_ref, ssm_ref,
                       cwx_ref, cwbc_ref, cbx_ref, cbbc_ref, dtb_ref, alog_ref, dexp_ref, nw_ref,
                       e_ref, eye_ref,
                       y_ref, convo_ref, ssmo_ref, st_ref):
    prev = conv_ref[0, 0]
    raw = jnp.concatenate([xr_ref[0], bcr_ref[0]], axis=1)
    cw = jnp.concatenate([cwx_ref[...], cwbc_ref[...]], axis=1)
    cbias = jnp.concatenate([cbx_ref[...], cbbc_ref[...]], axis=1)
    acc = cbias + raw * cw[CONV_WIDTH - 1:CONV_WIDTH, :]
    for k in range(CONV_WIDTH - 1):
        acc = acc + prev[k:k + 1, :] * cw[k:k + 1, :]
    convo_ref[0, 0, 0:CONV_WIDTH - 2, :] = prev[1:CONV_WIDTH - 1, :]
    convo_ref[0, 0, CONV_WIDTH - 2:CONV_WIDTH - 1, :] = raw
    xbc = _silu(acc)
    xs = xbc[:, 0:D_INNER]
    bm = xbc[:, D_INNER:D_INNER + SSM_GROUPS * SSM_STATE]
    cm = xbc[:, D_INNER + SSM_GROUPS * SSM_STATE:]

    dt = _softplus(dtr_ref[0] + dtb_ref[...])
    dec = jnp.exp(dt * -jnp.exp(alog_ref[...]))
    stacked = jnp.concatenate([dt, dec, jnp.zeros((SUBLANES - 2, LANES), F32)], axis=0)
    ex = _expand_dot(stacked, e_ref[...])
    dt_e = ex[0:1]
    dec_e = ex[1:2]
    xdt = xs * dt_e

    rows = jnp.concatenate([bm[:, g * SSM_STATE:(g + 1) * SSM_STATE] for g in range(SSM_GROUPS)]
                           + [cm[:, g * SSM_STATE:(g + 1) * SSM_STATE] for g in range(SSM_GROUPS)], axis=0)
    hi, mid, lo = _split3(rows)
    eye = eye_ref[...]
    cols = _dot_nt(eye, hi) + _dot_nt(eye, mid) + _dot_nt(eye, lo)

    for j in range(D_INNER // LANES):
        tile = ssm_ref[0, 0, 2 * j:2 * j + 2].reshape(LANES, SSM_STATE)
        st_ref[:, j * LANES:(j + 1) * LANES] = tile.T
    y_groups = []
    for g in range(SSM_GROUPS):
        g0 = g * GROUP_WIDTH
        new = (st_ref[:, g0:g0 + GROUP_WIDTH] * dec_e[:, g0:g0 + GROUP_WIDTH]
               + cols[:, g:g + 1] * xdt[:, g0:g0 + GROUP_WIDTH])
        st_ref[:, g0:g0 + GROUP_WIDTH] = new
        y_groups.append(jnp.sum(new * cols[:, SSM_GROUPS + g:SSM_GROUPS + g + 1], axis=0, keepdims=True))
    for j in range(D_INNER // LANES):
        tile = st_ref[:, j * LANES:(j + 1) * LANES].T
        ssmo_ref[0, 0, 2 * j:2 * j + 2] = tile.reshape(2, SSM_HEAD_DIM, SSM_STATE)
    y = jnp.concatenate(y_groups, axis=1) + xs * dexp_ref[...]

    hz = y * _silu(z_ref[0])
    normed = []
    for g in range(SSM_GROUPS):
        hg = hz[:, g * GROUP_WIDTH:(g + 1) * GROUP_WIDTH]
        normed.append(hg * lax.rsqrt(jnp.mean(hg * hg, axis=-1, keepdims=True) + RMS_EPS))
    y_ref[0] = jnp.concatenate(normed, axis=1) * nw_ref[...]


def _ssd_sample(zx, dts, conv_state, ssm_state, prm, layer, carried):
    nb = zx.shape[0]
    zx3 = zx.reshape(nb, 1, zx.shape[1])
    dt3 = dts.reshape(nb, 1, LANES)
    const = lambda shape: pl.BlockSpec(shape, lambda b: (0,) * len(shape))
    conv_blk = pl.BlockSpec((1, 1, CONV_WIDTH - 1, CONV_DIM), lambda b: (layer, b, 0, 0))
    ssm_blk = pl.BlockSpec((1, 1, SSM_HEADS, SSM_HEAD_DIM, SSM_STATE), lambda b: (layer, b, 0, 0, 0))
    ins = [zx3, zx3, zx3, dt3, conv_state, ssm_state, prm["cwx"], prm["cwbc"], prm["cbx"], prm["cbbc"],
           prm["dtb"], prm["alog"], prm["dexp"], prm["nw"], prm["expand2"], prm["eye"]]
    extra = list(carried)
    aliases = {len(ins): 1, len(ins) + 1: 2}
    y, conv_new, ssm_new = pl.pallas_call(
        _skip_carried(_ssd_sample_kernel, len(ins), len(extra)),
        grid=(nb,),
        in_specs=[pl.BlockSpec((1, 1, D_INNER), lambda b: (b, 0, 0)),
                  pl.BlockSpec((1, 1, D_INNER), lambda b: (b, 0, 1)),
                  pl.BlockSpec((1, 1, BC_WIDTH), lambda b: (b, 0, 4)),
                  pl.BlockSpec((1, 1, LANES), lambda b: (b, 0, 0)),
                  conv_blk, ssm_blk,
                  const((CONV_WIDTH, D_INNER)), const((CONV_WIDTH, BC_WIDTH)),
                  const((1, D_INNER)), const((1, BC_WIDTH)),
                  const((1, LANES)), const((1, LANES)),
                  const((1, D_INNER)), const((1, D_INNER)),
                  const((2 * LANES, D_INNER)), const((LANES, LANES))]
                 + [pl.BlockSpec(memory_space=pl.ANY)] * len(extra),
        out_specs=[pl.BlockSpec((1, 1, D_INNER), lambda b: (b, 0, 0)), conv_blk, ssm_blk],
        out_shape=[jax.ShapeDtypeStruct((nb, 1, D_INNER), F32),
                   jax.ShapeDtypeStruct(conv_state.shape, F32),
                   jax.ShapeDtypeStruct(ssm_state.shape, F32)],
        scratch_shapes=[pltpu.VMEM((SSM_STATE, D_INNER), F32)],
        input_output_aliases=aliases,
        compiler_params=_cparams(1),
    )(*ins, *extra)
    return y.reshape(nb, D_INNER), conv_new, ssm_new


def _b_in_proj_kernel(x_ref, w_ref, q0_ref, q1_ref, q2_ref, gate_ref, acc_ref):
    xb = x_ref[...].astype(BF16)
    for g, dst in enumerate((q0_ref, q1_ref, q2_ref)):
        dil = DIL_GROUPS[g][1]
        for c0 in range(0, ATT_WIDTH, MM_COLS):
            acc = _dot(xb, w_ref[:, g * ATT_WIDTH + c0:g * ATT_WIDTH + c0 + MM_COLS])
            if dil == 1:
                dst[0, 0, :, c0:c0 + MM_COLS] = acc.astype(dst.dtype)
            else:
                for cb in range(MM_COLS // LANES):
                    acc_ref[c0 // LANES + cb] = acc[:, cb * LANES:(cb + 1) * LANES]
        if dil > 1:
            _deinterleave(dst, acc_ref, dil)
    for c0 in range(0, ATT_WIDTH, PROJ_COLS):
        gate_ref[:, c0:c0 + PROJ_COLS] = _dot(xb, w_ref[:, N_DIL * ATT_WIDTH + c0:N_DIL * ATT_WIDTH + c0 + PROJ_COLS])


def _b_in_proj(x, w, bsz, seq):
    tm = PROJ_ROWS
    tpb = seq // tm
    m, k = x.shape
    qspecs, qshapes = [], []
    for _, dil in DIL_GROUPS:
        assert tm % (dil * 2 * SUBLANES) == 0
        qspecs.append(pl.BlockSpec((1, dil, tm // dil, ATT_WIDTH), lambda i: (i // tpb, 0, i % tpb, 0)))
        qshapes.append(jax.ShapeDtypeStruct((bsz, dil, seq // dil, ATT_WIDTH), BF16))
    return pl.pallas_call(
        _b_in_proj_kernel,
        grid=(m // tm,),
        in_specs=[pl.BlockSpec((tm, k), lambda i: (i, 0)),
                  pl.BlockSpec(w.shape, lambda i: (0, 0))],
        out_specs=qspecs + [pl.BlockSpec((tm, ATT_WIDTH), lambda i: (i, 0))],
        out_shape=qshapes + [jax.ShapeDtypeStruct((m, ATT_WIDTH), F32)],
        scratch_shapes=[pltpu.VMEM((ATT_WIDTH // LANES, tm, LANES), F32)],
        compiler_params=_cparams(1),
    )(x, w)


def _kv_proj_kernel(x_ref, w_ref, kv_ref, kvt_ref, acc_ref, *, dil, first_tile, wt):
    t = pl.program_id(1)
    xb = x_ref[...].astype(BF16)
    for c0 in range(0, KV_ROW, MM_COLS):
        acc = _dot(xb, w_ref[:, c0:c0 + MM_COLS])
        for cb in range(MM_COLS // LANES):
            acc_ref[c0 // LANES + cb] = acc[:, cb * LANES:(cb + 1) * LANES]
        if dil == 1:
            kv_ref[0, 0, :, c0:c0 + MM_COLS] = acc.astype(kv_ref.dtype)
    if dil > 1:
        _deinterleave(kv_ref, acc_ref, dil)

    @pl.when(t >= first_tile)
    def _():
        tm = acc_ref.shape[1]
        for cb in range(KV_ROW // LANES):
            for rb in range(wt // LANES):
                r0 = tm - wt + rb * LANES
                kvt_ref[0, cb * LANES:(cb + 1) * LANES, rb * LANES:(rb + 1) * LANES] = (
                    acc_ref[cb, r0:r0 + LANES, :].T)


def _kv_proj(x, w, g, bsz, seq):
    win, dil = DIL_GROUPS[g]
    win = min(win, seq)
    tm = KV_ROWS
    tpb = seq // tm
    wt = min(win, tm)
    first_tile = tpb - win // wt
    assert tm % (dil * 2 * SUBLANES) == 0 and win % wt == 0 and seq % tm == 0
    k = x.shape[1]
    return pl.pallas_call(
        functools.partial(_kv_proj_kernel, dil=dil, first_tile=first_tile, wt=wt),
        grid=(bsz, tpb),
        in_specs=[pl.BlockSpec((tm, k), lambda b, t: (b * tpb + t, 0)),
                  pl.BlockSpec((k, KV_ROW), lambda b, t: (0, 0))],
        out_specs=[pl.BlockSpec((1, dil, tm // dil, KV_ROW), lambda b, t: (b, 0, t, 0)),
                   pl.BlockSpec((1, KV_ROW, wt), lambda b, t: (b, 0, jnp.maximum(t - first_tile, 0)))],
        out_shape=[jax.ShapeDtypeStruct((bsz, dil, seq // dil, KV_ROW), BF16),
                   jax.ShapeDtypeStruct((bsz, KV_ROW, win), F32)],
        scratch_shapes=[pltpu.VMEM((KV_ROW // LANES, tm, LANES), F32)],
        compiler_params=_cparams(2),
    )(x, w)


def _attn_prompt_kernel(slope_ref, q_ref, kv_ref, o_ref, lse_ref, *scratch, dil, has_prev, units):
    if has_prev:
        j = pl.program_id(2)
        kvp_ref, = scratch

        @pl.when(j == 0)
        def _():
            kvp_ref[...] = jnp.zeros_like(kvp_ref)

        rows = [pl.ds(u * Q_BLOCK, Q_BLOCK) for u in range(units)]
        for u in range(units):
            prev = kvp_ref if u == 0 else kv_ref.at[0, 0, rows[u - 1]]
            _attn_block(slope_ref, q_ref.at[0, 0, rows[u]], kv_ref.at[0, 0, rows[u]], prev,
                        o_ref.at[0, 0, rows[u]], lse_ref.at[0, 0, rows[u]], (j == 0) if u == 0 else None, dil)
        kvp_ref[...] = kv_ref[0, 0, rows[units - 1]]
    else:
        for u in range(units):
            _attn_block(slope_ref, q_ref.at[0, u], kv_ref.at[0, u], None, o_ref.at[0, u], lse_ref.at[0, u], None, dil)


def _attn_block(slope_ref, q_ref, kvc_ref, kvp_ref, o_ref, lse_ref, first, dil):
    has_prev = kvp_ref is not None
    qb = Q_BLOCK
    q = (q_ref[...].astype(F32) * ATT_SCALE).astype(BF16)
    row = lax.broadcasted_iota(jnp.int32, (qb, qb), 0)
    col = lax.broadcasted_iota(jnp.int32, (qb, qb), 1)
    lower = col <= row
    diag = col == row
    dist = (((row - col) & (qb - 1)) * dil).astype(F32)
    if not has_prev:
        dist = jnp.where(lower, dist, jnp.inf)
    elif first is not None:
        dist = jnp.where(jnp.logical_or(lower, jnp.logical_not(first)), dist, jnp.inf)
    far = float(qb * dil)
    lane = lax.broadcasted_iota(jnp.int32, (qb, LANES), 1)
    lane_lo = lane < ATT_HEAD_DIM
    mx_tile = jnp.zeros((qb, LANES), F32)
    den_tile = jnp.ones((qb, LANES), F32)

    for hb in range(0, ATT_HEADS, ATT_HEAD_BATCH):
        heads = list(range(hb, hb + ATT_HEAD_BATCH))
        kc, vc, kp, vp = {}, {}, {}, {}
        for pr in sorted({h // 2 for h in heads}):
            sl = slice(pr * LANES, (pr + 1) * LANES)
            vsl = slice(ATT_WIDTH + pr * LANES, ATT_WIDTH + (pr + 1) * LANES)
            kc[pr] = kvc_ref[:, sl]
            vc[pr] = kvc_ref[:, vsl]
            if has_prev:
                kp[pr] = kvp_ref[:, sl]
                vp[pr] = kvp_ref[:, vsl]
        qh = [jnp.where(lane_lo if h % 2 == 0 else jnp.logical_not(lane_lo),
                        q[:, (h // 2) * LANES:(h // 2 + 1) * LANES], jnp.zeros((qb, LANES), BF16)) for h in heads]
        slopes = [slope_ref[h] for h in heads]
        s_c = [_dot_nt(qh[i], kc[h // 2]) for i, h in enumerate(heads)]
        if has_prev:
            s_p = [_dot_nt(qh[i], kp[h // 2]) for i, h in enumerate(heads)]
            s = [jnp.where(lower, s_c[i], s_p[i]) - slopes[i] * dist for i in range(len(heads))]
            s_d = [jnp.sum(jnp.where(diag, s_p[i], 0.0), axis=-1, keepdims=True) - slopes[i] * far
                   for i in range(len(heads))]
            if first is not None:
                s_d = [jnp.where(first, -jnp.inf, sd) for sd in s_d]
            mx = [jnp.maximum(jnp.max(s[i], axis=-1, keepdims=True), s_d[i]) for i in range(len(heads))]
            p = [jnp.exp(s[i] - mx[i]) for i in range(len(heads))]
            p_d = [jnp.exp(s_d[i] - mx[i]) for i in range(len(heads))]
            den = [jnp.sum(p[i], axis=-1, keepdims=True) + p_d[i] for i in range(len(heads))]
            acc = [_dot(jnp.where(lower, p[i], 0.0).astype(BF16), vc[h // 2])
                   + _dot(jnp.where(lower, jnp.where(diag, p_d[i], 0.0), p[i]).astype(BF16), vp[h // 2])
                   for i, h in enumerate(heads)]
        else:
            s = [s_c[i] - slopes[i] * dist for i in range(len(heads))]
            mx = [jnp.max(s[i], axis=-1, keepdims=True) for i in range(len(heads))]
            p = [jnp.exp(s[i] - mx[i]) for i in range(len(heads))]
            den = [jnp.sum(p[i], axis=-1, keepdims=True) for i in range(len(heads))]
            acc = [_dot(p[i].astype(BF16), vc[h // 2]) for i, h in enumerate(heads)]
        out = [acc[i] * (1.0 / den[i]) for i in range(len(heads))]
        for i, h in enumerate(heads):
            mx_tile = jnp.where(lane == h, mx[i], mx_tile)
            den_tile = jnp.where(lane == h, den[i], den_tile)
        for i in range(0, len(heads), 2):
            pr = heads[i] // 2
            o_ref[:, pr * LANES:(pr + 1) * LANES] = jnp.where(lane_lo, out[i], out[i + 1]).astype(o_ref.dtype)
    lse_ref[...] = mx_tile + jnp.log(den_tile)


def _attn_prompt_group(q, kv, slopes, g):
    win, dil = DIL_GROUPS[g]
    bsz, _, n, _ = q.shape
    assert win // dil == Q_BLOCK and n % Q_BLOCK == 0
    nblk = n // Q_BLOCK
    has_prev = nblk > 1
    units = min(ATT_UNITS, nblk if has_prev else dil)
    if has_prev:
        assert nblk % units == 0
        grid = (bsz, dil, nblk // units)
        blk = lambda width: pl.BlockSpec((1, 1, units * Q_BLOCK, width), lambda b, r, j: (b, r, j, 0))
    else:
        assert dil % units == 0
        grid = (bsz, dil // units, 1)
        blk = lambda width: pl.BlockSpec((1, units, Q_BLOCK, width), lambda b, r, j: (b, r, 0, 0))
    return pl.pallas_call(
        functools.partial(_attn_prompt_kernel, dil=dil, has_prev=has_prev, units=units),
        grid=grid,
        in_specs=[pl.BlockSpec(memory_space=pltpu.SMEM), blk(ATT_WIDTH), blk(KV_ROW)],
        out_specs=[blk(ATT_WIDTH), blk(LANES)],
        out_shape=[jax.ShapeDtypeStruct((bsz, dil, n, ATT_WIDTH), BF16),
                   jax.ShapeDtypeStruct((bsz, dil, n, LANES), F32)],
        scratch_shapes=[pltpu.VMEM((Q_BLOCK, KV_ROW), BF16)] if has_prev else [],
        compiler_params=_cparams(3),
    )(slopes, q, kv)


def _merge_out_kernel(o0_ref, o1_ref, o2_ref, l0_ref, l1_ref, l2_ref, gate_ref, e_ref, w_ref, r_ref,
                      g_ref, b_ref, out_ref, os_ref, ls_ref, lhs_ref, og_ref):
    for g, (o_ref, l_ref) in enumerate(((o0_ref, l0_ref), (o1_ref, l1_ref), (o2_ref, l2_ref))):
        dil = DIL_GROUPS[g][1]
        if dil > 1:
            _interleave(ls_ref.at[g - 1], l_ref, dil)
            _interleave(os_ref.at[g - 1], o_ref, dil)
    l0, l1, l2 = l0_ref[0, 0], ls_ref[0, 0], ls_ref[1, 0]
    top = jnp.maximum(jnp.maximum(l0, l1), l2)
    w0, w1, w2 = jnp.exp(l0 - top), jnp.exp(l1 - top), jnp.exp(l2 - top)
    inv = 1.0 / (w0 + w1 + w2)
    for i, wn in enumerate((w0 * inv, w1 * inv)):
        hi = wn.astype(BF16)
        lhs_ref[i, :, 0:LANES] = hi
        lhs_ref[i, :, LANES:2 * LANES] = (wn - hi.astype(F32)).astype(BF16)
    for c0 in range(0, ATT_WIDTH, MM_COLS):
        sl = slice(c0, c0 + MM_COLS)
        blocks = range(c0 // LANES, (c0 + MM_COLS) // LANES)
        w0e = _dot(lhs_ref[0], e_ref[:, sl])
        w1e = _dot(lhs_ref[1], e_ref[:, sl])
        o1 = jnp.concatenate([os_ref[0, cb] for cb in blocks], axis=1)
        o2 = jnp.concatenate([os_ref[1, cb] for cb in blocks], axis=1)
        o = w0e * o0_ref[0, 0, :, sl].astype(F32) + w1e * o1 + (1.0 - w0e - w1e) * o2
        og_ref[:, sl] = (o * _silu(gate_ref[:, sl])).astype(BF16)
    v = DEEPNORM_ALPHA * r_ref[...] + _dot(og_ref[...], w_ref[...])
    out_ref[...] = _layer_norm(v, g_ref[...], b_ref[...])


def _merge_out(os_, lses, gate, expand16, w, resid, g, b, seq):
    tm = PROJ_ROWS
    tpb = seq // tm
    m = resid.shape[0]
    rowblk = lambda width: pl.BlockSpec((tm, width), lambda i: (i, 0))
    const = lambda shape: pl.BlockSpec(shape, lambda i: (0, 0))
    resblk = lambda dil, width: pl.BlockSpec((1, dil, tm // dil, width), lambda i: (i // tpb, 0, i % tpb, 0))
    return pl.pallas_call(
        _merge_out_kernel,
        grid=(m // tm,),
        in_specs=[resblk(dil, ATT_WIDTH) for _, dil in DIL_GROUPS] + [resblk(dil, LANES) for _, dil in DIL_GROUPS]
                 + [rowblk(ATT_WIDTH), const((2 * LANES, ATT_WIDTH)), const((ATT_WIDTH, D_MODEL)), rowblk(D_MODEL),
                    const((1, D_MODEL)), const((1, D_MODEL))],
        out_specs=rowblk(D_MODEL),
        out_shape=jax.ShapeDtypeStruct((m, D_MODEL), F32),
        scratch_shapes=[pltpu.VMEM((N_DIL - 1, ATT_WIDTH // LANES, tm, LANES), F32),
                        pltpu.VMEM((N_DIL - 1, 1, tm, LANES), F32),
                        pltpu.VMEM((2, tm, 2 * LANES), BF16),
                        pltpu.VMEM((tm, ATT_WIDTH), BF16)],
        compiler_params=_cparams(1),
    )(*os_, *lses, gate, expand16, w, resid, g, b)


def _cache_shift_block(c_ref, sel_ref, o_ref, comp_ref, dil):
    x = c_ref[...]
    if dil > 1:
        comp_ref[...] = _dot(x.astype(BF16), sel_ref[...]).astype(BF16)
    else:
        comp_ref[...] = x.astype(BF16)
    o_ref[...] = pltpu.roll(x, x.shape[1] - 1, axis=1)


def _cache_rider(cache, g, steps):
    win, dil = DIL_GROUPS[g]
    nb = cache.shape[0]
    assert cache.shape[1] == win and win // dil == Q_BLOCK and (nb * KV_ROW) % (steps * SUBLANES) == 0
    rows = nb * KV_ROW // steps
    ct = cache.transpose(0, 2, 3, 4, 1).reshape(nb * KV_ROW, win)
    sel = (jnp.arange(win)[:, None] == jnp.arange(Q_BLOCK)[None, :] * dil).astype(BF16)
    return dict(dil=dil, ins=[ct, sel], rows=rows, win=win,
                out_shape=[jax.ShapeDtypeStruct((nb * KV_ROW, win), cache.dtype),
                           jax.ShapeDtypeStruct((nb * KV_ROW, Q_BLOCK), BF16)])


def _cache_append_kernel(*refs):
    new_ref = refs[N_DIL]
    first = pl.program_id(0) * APPEND_SEQS
    batch = lax.broadcasted_iota(jnp.int32, (KV_ROW, new_ref.shape[1]), 1)
    last = lax.broadcasted_iota(jnp.int32, (KV_ROW, LANES), 1) == LANES - 1
    for g in range(N_DIL):
        c_ref, o_ref = refs[g], refs[N_DIL + 1 + g]
        nv = new_ref[g * KV_ROW:(g + 1) * KV_ROW, :]
        for s in range(APPEND_SEQS):
            newcol = jnp.sum(jnp.where(batch == first + s, nv, 0.0), axis=1, keepdims=True)
            rows = slice(s * KV_ROW, (s + 1) * KV_ROW)
            o_ref[rows, :] = jnp.where(last, newcol, c_ref[rows, :])


def _cache_append(shifted, kvt_new, nb):
    assert nb % APPEND_SEQS == 0
    blocks = [pl.BlockSpec((APPEND_SEQS * KV_ROW, LANES), lambda b, s=s: (b, s.shape[1] // LANES - 1))
              for s in shifted]
    outs = pl.pallas_call(
        _cache_append_kernel,
        grid=(nb // APPEND_SEQS,),
        in_specs=blocks + [pl.BlockSpec(kvt_new.shape, lambda b: (0, 0))],
        out_specs=blocks,
        out_shape=[jax.ShapeDtypeStruct(s.shape, s.dtype) for s in shifted],
        input_output_aliases={g: g for g in range(N_DIL)},
        compiler_params=_cparams(1),
    )(*shifted, kvt_new)
    return [o.reshape(nb, 2, ATT_HEADS, ATT_HEAD_DIM, o.shape[1]).transpose(0, 4, 1, 2, 3) for o in outs]


def _attn_sample_kernel(slope_ref, q0_ref, q1_ref, q2_ref, gate_ref, n0_ref, n1_ref, n2_ref,
                        c0_ref, c1_ref, c2_ref, o_ref):
    nh = ATT_HEADS
    keys = Q_BLOCK
    lane = lax.broadcasted_iota(jnp.int32, (nh, ATT_WIDTH), 1)
    hrow = lax.broadcasted_iota(jnp.int32, (nh, ATT_WIDTH), 0)
    head_mask = (lane // ATT_HEAD_DIM) == hrow
    kidx = lax.broadcasted_iota(jnp.int32, (nh, keys), 1)
    slope = slope_ref[:, 0:1]
    outs, lses = [], []
    for g, (q_ref, n_ref, c_ref) in enumerate(((q0_ref, n0_ref, c0_ref), (q1_ref, n1_ref, c1_ref),
                                               (q2_ref, n2_ref, c2_ref))):
        dil = DIL_GROUPS[g][1]
        q = q_ref[0] * ATT_SCALE
        qm = jnp.where(head_mask, jnp.broadcast_to(q, (nh, ATT_WIDTH)), 0.0)
        new = n_ref[0]
        k_t = c_ref[0:ATT_WIDTH, :]
        v_t = c_ref[ATT_WIDTH:KV_ROW, :]
        dist = ((keys - kidx) * dil).astype(F32)
        s = _dot(qm.astype(BF16), k_t) - slope * dist
        s_new = jnp.sum(qm * new[:, 0:ATT_WIDTH], axis=-1, keepdims=True)
        mx = jnp.maximum(jnp.max(s, axis=-1, keepdims=True), s_new)
        p = jnp.exp(s - mx)
        p_new = jnp.exp(s_new - mx)
        den = jnp.sum(p, axis=-1, keepdims=True) + p_new
        outs.append((_dot_nt(p.astype(BF16), v_t) + p_new * new[:, ATT_WIDTH:]) / den)
        lses.append(mx + jnp.log(den))
    top = jnp.maximum(jnp.maximum(lses[0], lses[1]), lses[2])
    ws = [jnp.exp(l - top) for l in lses]
    o = (ws[0] * outs[0] + ws[1] * outs[1] + ws[2] * outs[2]) / (ws[0] + ws[1] + ws[2])
    o = jnp.sum(jnp.where(head_mask, o, 0.0), axis=0, keepdims=True)
    o_ref[0] = o * _silu(gate_ref[0])


def _attn_sample(proj, kv_new, comps, slopes_b):
    nb = proj.shape[0]
    proj3 = proj.reshape(nb, 1, 4 * ATT_WIDTH)
    kvn3 = kv_new.reshape(nb, 1, N_DIL * KV_ROW)
    qspec = lambda g: pl.BlockSpec((1, 1, ATT_WIDTH), lambda b: (b, 0, g))
    nspec = lambda g: pl.BlockSpec((1, 1, KV_ROW), lambda b: (b, 0, g))
    cspec = pl.BlockSpec((KV_ROW, Q_BLOCK), lambda b: (b, 0))
    o = pl.pallas_call(
        _attn_sample_kernel,
        grid=(nb,),
        in_specs=[pl.BlockSpec((ATT_HEADS, LANES), lambda b: (0, 0)),
                  qspec(0), qspec(1), qspec(2), qspec(3), nspec(0), nspec(1), nspec(2),
                  cspec, cspec, cspec],
        out_specs=pl.BlockSpec((1, 1, ATT_WIDTH), lambda b: (b, 0, 0)),
        out_shape=jax.ShapeDtypeStruct((nb, 1, ATT_WIDTH), F32),
        compiler_params=_cparams(1),
    )(slopes_b, proj3, proj3, proj3, proj3, kvn3, kvn3, kvn3, *comps)
    return o.reshape(nb, ATT_WIDTH)


def _pad_lanes(v):
    return jnp.pad(v.astype(F32), (0, LANES - v.shape[0])).reshape(1, LANES)


def kernel(x_prompt, x_sample, state_ssm, state_conv, cache_kv_w128, cache_kv_w512, cache_kv_w2048,
           a_in_proj, a_conv_w, a_conv_b, a_dt_bias, a_log, a_d, a_norm_w, a_out_proj,
           kv_proj, b_in_proj, b_out_proj, ln_g, ln_b):
    bsz, seq, _ = x_prompt.shape
    nb = x_sample.shape[0]
    assert x_sample.shape[1] == 1 and seq % CHUNK == 0
    caches = (cache_kv_w128, cache_kv_w512, cache_kv_w2048)

    heads = jnp.arange(LANES)[:, None]
    expand32 = (heads == jnp.arange(D_INNER)[None, :] // SSM_HEAD_DIM).astype(BF16)
    expand16 = (heads == jnp.arange(ATT_WIDTH)[None, :] // ATT_HEAD_DIM).astype(BF16)
    expand32 = jnp.concatenate([expand32, expand32], axis=0)
    expand16 = jnp.concatenate([expand16, expand16], axis=0)
    tril = (jnp.arange(CHUNK)[:, None] >= jnp.arange(CHUNK)[None, :]).astype(BF16)
    eye = jnp.eye(LANES, dtype=BF16)
    p = jnp.arange(CONV_WIDTH * CHUNK)
    blk = CONV_WIDTH * SUBLANES
    tap_i = (p // blk) * SUBLANES + p % SUBLANES
    tap_k = (p % blk) // SUBLANES
    shift = (jnp.arange(2 * CHUNK)[None, :] == (CHUNK + tap_i - (CONV_WIDTH - 1 - tap_k))[:, None]).astype(BF16)
    slopes =jnp.exp2(-8.0 * jnp.arange(1, ATT_HEADS + 1, dtype=F32) / ATT_HEADS)
    slopes_b = jnp.broadcast_to(slopes[:, None], (ATT_HEADS, LANES))

    hp = x_prompt.reshape(bsz * seq, D_MODEL)
    hs = x_sample.reshape(nb, D_MODEL)
    stacks_p = [jnp.zeros((N_A_LAYERS, bsz, SSM_HEADS, SSM_HEAD_DIM, SSM_STATE), F32),
                jnp.zeros((N_A_LAYERS, bsz, CONV_WIDTH - 1, CONV_DIM), F32)]
    stacks_s = [jnp.zeros(state_conv.shape, F32), jnp.zeros(state_ssm.shape, F32)]
    steps = bsz * (seq // CHUNK)
    riders = {0: [_cache_rider(caches[2], 2, steps)],
              1: [_cache_rider(caches[1], 1, steps), _cache_rider(caches[0], 0, steps)]}
    ride_groups = {0: [2], 1: [1, 0]}
    shifted, comps = [None] * N_DIL, [None] * N_DIL

    for i in range(N_A_LAYERS):
        w_in = a_in_proj[i]
        w_main = w_in[:, 0:D_INNER + CONV_DIM].astype(BF16)
        w_dt = jnp.pad(w_in[:, D_INNER + CONV_DIM:], ((0, 0), (0, LANES - SSM_HEADS))).astype(BF16)
        w_out = a_out_proj[i].astype(BF16)
        prm = dict(
            cwx=a_conv_w[i][:, 0:D_INNER], cwbc=a_conv_w[i][:, D_INNER:],
            cbx=a_conv_b[i][0:D_INNER].reshape(1, D_INNER), cbbc=a_conv_b[i][D_INNER:].reshape(1, BC_WIDTH),
            dtb=_pad_lanes(a_dt_bias[i]), alog=_pad_lanes(a_log[i]),
            dexp=jnp.repeat(a_d[i].astype(F32), SSM_HEAD_DIM).reshape(1, D_INNER),
            nw=a_norm_w[i].reshape(1, D_INNER), expand2=expand32, tril=tril, eye=eye, shift=shift)
        g_ln, b_ln = ln_g[i].reshape(1, D_MODEL), ln_b[i].reshape(1, D_MODEL)

        zx, dtp = _a_in_proj(hp, w_main, w_dt, BF16)
        yn, *rest = _ssd_prompt(zx, dtp, prm, bsz, seq, i, stacks_p, riders.get(i, []))
        stacks_p = rest[0:2]
        for n, g in enumerate(ride_groups.get(i, [])):
            shifted[g], comps[g] = rest[2 + 2 * n], rest[3 + 2 * n]
        hp = _matmul_ln(yn, w_out, hp, g_ln, b_ln)

        zx_s, dt_s = _a_in_proj(hs, w_main, w_dt, F32)
        yn_s, *stacks_s = _ssd_sample(zx_s, dt_s, state_conv, state_ssm, prm, i, stacks_s)
        hs = _matmul_ln(yn_s, w_out, hs, g_ln, b_ln)
    ssm_p, conv_p = stacks_p
    conv_s, ssm_s = stacks_s

    kvw = kv_proj.reshape(D_MODEL, 2, N_DIL, ATT_WIDTH)
    kvw_g = [jnp.concatenate([kvw[:, 0, g], kvw[:, 1, g]], axis=1).astype(BF16) for g in range(N_DIL)]
    kvw_all = jnp.concatenate(kvw_g, axis=1)
    kv_p, new_kv_p = [], []
    for g in range(N_DIL):
        kv_res, kv_t = _kv_proj(hp, kvw_g[g], g, bsz, seq)
        kv_p.append(kv_res)
        new_kv_p.append(kv_t.reshape(bsz, 2, ATT_HEADS, ATT_HEAD_DIM, kv_t.shape[-1]).transpose(0, 4, 1, 2, 3))
    kv_s = _matmul(hs, kvw_all, tn=1024)
    kvt_s = _matmul_nt(kvw_all.T, hs, tn=1024)
    new_kv_s = _cache_append(shifted, kvt_s, nb)

    for j in range(N_B_LAYERS):
        layer = N_A_LAYERS + j
        w_in = b_in_proj[j].astype(BF16)
        w_out = b_out_proj[j].astype(BF16)
        g_ln, b_ln = ln_g[layer].reshape(1, D_MODEL), ln_b[layer].reshape(1, D_MODEL)

        q0, q1, q2, gate = _b_in_proj(hp, w_in, bsz, seq)
        res = [_attn_prompt_group(q, kv_p[g], slopes, g) for g, q in enumerate((q0, q1, q2))]
        hp = _merge_out([r[0] for r in res], [r[1] for r in res], gate, expand16, w_out, hp, g_ln, b_ln, seq)

        proj_s = _matmul(hs, w_in, tn=1024)
        og_s = _attn_sample(proj_s, kv_s, comps, slopes_b)
        hs = _matmul_ln(og_s, w_out, hs, g_ln, b_ln)

    return (hp.reshape(bsz, seq, D_MODEL), hs.reshape(nb, 1, D_MODEL),
            ssm_p, conv_p, new_kv_p[0], new_kv_p[1], new_kv_p[2],
            ssm_s, conv_s, new_kv_s[0], new_kv_s[1], new_kv_s[2])
```
